```python
import jax, jax.numpy as jnp
from jax import lax
import numpy as np

D_MODEL = 1024
BATCH = 4
SEQ = 8192
DEPTH = 2
DEC_BATCH = 128
DEC_SEQ = 1
PAST_LEN = 16384
PAGE_SIZE = 128

N_EVEN = (DEPTH + 1) // 2
N_ODD = DEPTH // 2
D_CONV = D_MODEL // 2
CONV_W = 3
H_MLA = 8
Q_RANK = D_MODEL // 2
KV_RANK = D_MODEL // 4
NOPE_DIM = 64
ROPE_DIM = 32
V_DIM = 64
ROPE_THETA = 10000.0
MLA_SCALE = (NOPE_DIM + ROPE_DIM) ** -0.5
H_SB = 16
N_KV_SB = 4
HD_SB = 64
G_SB = H_SB // N_KV_SB
SB_SCALE = HD_SB ** -0.5
BLOCK_Q = 128
IN0_WIDTHS = (D_CONV, D_CONV, D_CONV, D_CONV, Q_RANK, KV_RANK, ROPE_DIM, H_MLA * V_DIM)
IN0 = sum(IN0_WIDTHS)
MIX0 = D_CONV + H_MLA * V_DIM
MIX1 = H_SB * HD_SB
IN1_WIDTHS = (MIX1, N_KV_SB * HD_SB, N_KV_SB * HD_SB, MIX1)
IN1 = sum(IN1_WIDTHS)
DEEPNORM_ALPHA = (2 * DEPTH) ** 0.25
DEEPNORM_BETA = (8 * DEPTH) ** -0.25
EPS = 1e-5

kernel_name = 'hybrid_conv_mla_stickbreaking_step'


def split_cols(x, widths):
    out, start = [], 0
    for w in widths:
        out.append(x[..., start:start + w])
        start += w
    return out


def layer_norm(x, g, b):
    xf = x.astype(jnp.float32)
    mu = xf.mean(-1, keepdims=True)
    var = jnp.square(xf - mu).mean(-1, keepdims=True)
    return ((xf - mu) * lax.rsqrt(var + EPS) * g + b).astype(x.dtype)


def rms_norm(x, g):
    xf = x.astype(jnp.float32)
    return (xf * lax.rsqrt(jnp.square(xf).mean(-1, keepdims=True) + EPS) * g).astype(x.dtype)


def rope(x, pos):
    half = x.shape[-1] // 2
    freq = ROPE_THETA ** (-jnp.arange(half, dtype=jnp.float32) / half)
    ang = pos.astype(jnp.float32)[:, None] * freq[None, :]
    shape = (1, ang.shape[0]) + (1,) * (x.ndim - 3) + (half,)
    cos, sin = jnp.cos(ang).reshape(shape), jnp.sin(ang).reshape(shape)
    xf = x.astype(jnp.float32)
    x1, x2 = xf[..., :half], xf[..., half:]
    return jnp.concatenate([x1 * cos - x2 * sin, x2 * cos + x1 * sin], -1).astype(x.dtype)


def short_conv(u_pad, w):
    t = u_pad.shape[1] - (CONV_W - 1)
    out = w[0] * u_pad[:, 0:t]
    for j in range(1, CONV_W):
        out = out + w[j] * u_pad[:, j:j + t]
    return out


def sb_weights(z, mask):
    log_beta = jax.nn.log_sigmoid(z)
    log_keep = jnp.where(mask, jax.nn.log_sigmoid(-z), 0.0)
    later = lax.cumsum(log_keep, axis=z.ndim - 1, reverse=True) - log_keep
    return jnp.where(mask, jnp.exp(log_beta + later), 0.0)


def even_in(x, pos, w_in, q_norm, w_qb, kv_norm):
    b, t, _ = x.shape
    h, gc, gb, za, qa, kva, kpe, zb = split_cols(x @ w_in, IN0_WIDTHS)
    q = (rms_norm(qa, q_norm) @ w_qb).reshape(b, t, H_MLA, NOPE_DIM + ROPE_DIM)
    q_nope, q_pe = q[..., :NOPE_DIM], rope(q[..., NOPE_DIM:], pos)
    ckv = rms_norm(kva, kv_norm)
    kpe = rope(kpe, pos)
    return gc * h, gb, za, q_nope, q_pe, ckv, kpe, zb


def even_out(x, conv_out, gb, za, attn, zb, w_out, g, b):
    bsz, t, _ = x.shape
    ya = gb * conv_out * jax.nn.silu(za)
    yb = attn.reshape(bsz, t, H_MLA * V_DIM) * jax.nn.silu(zb)
    y = jnp.concatenate([ya, yb], -1) @ w_out
    return layer_norm(DEEPNORM_ALPHA * x + y, g, b)


def mla_prompt(q_nope, q_pe, ckv, kpe, w_uk, w_uv):
    bsz, s = ckv.shape[:2]
    k_nope = jnp.einsum('bsr,rhn->bshn', ckv, w_uk)
    v = jnp.einsum('bsr,rhv->bshv', ckv, w_uv)
    kpos = jnp.arange(s)

    def block(i):
        q0 = i * BLOCK_Q
        qn = lax.dynamic_slice_in_dim(q_nope, q0, BLOCK_Q, 1)
        qp = lax.dynamic_slice_in_dim(q_pe, q0, BLOCK_Q, 1)
        sc = (jnp.einsum('bqhn,bshn->bhqs', qn, k_nope)
              + jnp.einsum('bqhe,bse->bhqs', qp, kpe)).astype(jnp.float32) * MLA_SCALE
        mask = kpos[None, :] <= (q0 + jnp.arange(BLOCK_Q))[:, None]
        p = jax.nn.softmax(jnp.where(mask, sc, -jnp.inf), axis=-1)
        return jnp.einsum('bhqs,bshv->bqhv', p.astype(v.dtype), v)

    o = lax.map(block, jnp.arange(s // BLOCK_Q))
    return jnp.moveaxis(o, 0, 1).reshape(bsz, s, H_MLA, V_DIM)


def mla_sample(q_nope, q_pe, ckv, kpe, ckv_past, kpe_past, w_uk, w_uv):
    t = ckv.shape[1]
    p_len = ckv_past.shape[1]
    q_lat = jnp.einsum('bthn,rhn->bthr', q_nope, w_uk)
    s_past = (jnp.einsum('bthr,bsr->bhts', q_lat, ckv_past)
              + jnp.einsum('bthe,bse->bhts', q_pe, kpe_past)).astype(jnp.float32)
    s_new = (jnp.einsum('bthr,bsr->bhts', q_lat, ckv)
             + jnp.einsum('bthe,bse->bhts', q_pe, kpe)).astype(jnp.float32)
    causal = jnp.arange(t)[None, :] <= jnp.arange(t)[:, None]
    sc = jnp.concatenate([s_past, jnp.where(causal, s_new, -jnp.inf)], -1) * MLA_SCALE
    p = jax.nn.softmax(sc, axis=-1).astype(ckv.dtype)
    o_lat = (jnp.einsum('bhts,bsr->bthr', p[..., :p_len], ckv_past)
             + jnp.einsum('bhts,bsr->bthr', p[..., p_len:], ckv))
    return jnp.einsum('bthr,rhv->bthv', o_lat, w_uv)


def odd_in(x, w_in):
    b, t, _ = x.shape
    q, k, v, z = split_cols(x @ w_in, IN1_WIDTHS)
    return (q.reshape(b, t, N_KV_SB, G_SB, HD_SB), k.reshape(b, t, N_KV_SB, HD_SB),
            v.reshape(b, t, N_KV_SB, HD_SB), z)


def odd_out(x, o, z, w_out, g, b):
    bsz, t, _ = x.shape
    y = (o.reshape(bsz, t, MIX1) * jax.nn.silu(z)) @ w_out
    return layer_norm(DEEPNORM_ALPHA * x + y, g, b)


def sb_prompt(q, k, v):
    bsz, s = k.shape[:2]
    kpos = jnp.arange(s)

    def block(i):
        q0 = i * BLOCK_Q
        qb = lax.dynamic_slice_in_dim(q, q0, BLOCK_Q, 1)
        z = jnp.einsum('bqkgd,bskd->bkgqs', qb, k).astype(jnp.float32) * SB_SCALE
        mask = kpos[None, :] < (q0 + jnp.arange(BLOCK_Q))[:, None]
        a = sb_weights(z, mask).astype(v.dtype)
        return jnp.einsum('bkgqs,bskd->bqkgd', a, v)

    o = lax.map(block, jnp.arange(s // BLOCK_Q))
    return jnp.moveaxis(o, 0, 1).reshape(bsz, s, N_KV_SB, G_SB, HD_SB)


def sb_sample(q, k, v, k_past, v_past):
    t = k.shape[1]
    p_len = k_past.shape[1]
    z = jnp.concatenate([jnp.einsum('btkgd,bskd->bkgts', q, k_past),
                         jnp.einsum('btkgd,bskd->bkgts', q, k)], -1).astype(jnp.float32) * SB_SCALE
    mask = jnp.arange(p_len + t)[None, :] < (p_len + jnp.arange(t))[:, None]
    a = sb_weights(z, mask).astype(v.dtype)
    return (jnp.einsum('bkgts,bskd->btkgd', a[..., :p_len], v_past)
            + jnp.einsum('bkgts,bskd->btkgd', a[..., p_len:], v))


def setup_inputs(seed: int = 0) -> dict:
    key = jax.random.key(seed)
    ks = iter(jax.random.split(key, 32))
    n_pages = PAST_LEN // PAGE_SIZE
    n_used = DEC_BATCH * n_pages
    n_phys = n_used + max(1, n_used // 4)

    def nrm(shape, scale=1.0):
        return jax.random.normal(next(ks), shape, jnp.float32) * scale

    page_table = jax.random.permutation(next(ks), n_phys)[:n_used].reshape(DEC_BATCH, n_pages).astype(jnp.int32)
    return {
        'x_prompt': nrm((BATCH, SEQ, D_MODEL)),
        'x_sample': nrm((DEC_BATCH, DEC_SEQ, D_MODEL)),
        'state_conv': nrm((N_EVEN, DEC_BATCH, CONV_W - 1, D_CONV)),
        'cache_ckv': nrm((N_EVEN, n_phys, PAGE_SIZE, KV_RANK)),
        'cache_kpe': nrm((N_EVEN, n_phys, PAGE_SIZE, ROPE_DIM)),
        'cache_k': nrm((N_ODD, n_phys, PAGE_SIZE, N_KV_SB, HD_SB)),
        'cache_v': nrm((N_ODD, n_phys, PAGE_SIZE, N_KV_SB, HD_SB)),
        'page_table': page_table,
        'w_in0': nrm((N_EVEN, D_MODEL, IN0), D_MODEL ** -0.5),
        'conv_w': nrm((N_EVEN, CONV_W, D_CONV), 0.5),
        'q_norm': 1.0 + nrm((N_EVEN, Q_RANK), 0.02),
        'w_qb': nrm((N_EVEN, Q_RANK, H_MLA * (NOPE_DIM + ROPE_DIM)), Q_RANK ** -0.5),
        'kv_norm': 1.0 + nrm((N_EVEN, KV_RANK), 0.02),
        'w_uk': nrm((N_EVEN, KV_RANK, H_MLA, NOPE_DIM), KV_RANK ** -0.5),
        'w_uv': nrm((N_EVEN, KV_RANK, H_MLA, V_DIM), KV_RANK ** -0.5),
        'w_out0': nrm((N_EVEN, MIX0, D_MODEL), DEEPNORM_BETA * MIX0 ** -0.5),
        'ln0_g': 1.0 + nrm((N_EVEN, D_MODEL), 0.02),
        'ln0_b': nrm((N_EVEN, D_MODEL), 0.02),
        'w_in1': nrm((N_ODD, D_MODEL, IN1), D_MODEL ** -0.5),
        'w_out1': nrm((N_ODD, MIX1, D_MODEL), DEEPNORM_BETA * MIX1 ** -0.5),
        'ln1_g': 1.0 + nrm((N_ODD, D_MODEL), 0.02),
        'ln1_b': nrm((N_ODD, D_MODEL), 0.02),
    }


def reference(x_prompt, x_sample, state_conv, cache_ckv, cache_kpe, cache_k, cache_v, page_table,
              w_in0, conv_w, q_norm, w_qb, kv_norm, w_uk, w_uv, w_out0, ln0_g, ln0_b,
              w_in1, w_out1, ln1_g, ln1_b):
    bsz, s_len = x_prompt.shape[:2]
    t_len = x_sample.shape[1]
    dbsz, n_pages = page_table.shape
    p_len = n_pages * PAGE_SIZE
    pos_p = jnp.arange(s_len)
    pos_s = p_len + jnp.arange(t_len)
    xp, xs = x_prompt, x_sample
    conv_p, conv_s, ckv_p, ckv_s, kpe_p, kpe_s = [], [], [], [], [], []
    k_p, k_s, v_p, v_s = [], [], [], []
    for layer in range(DEPTH):
        i = layer // 2
        if layer % 2 == 0:
            u, gb, za, qn, qp, ckv, kpe, zb = even_in(xp, pos_p, w_in0[i], q_norm[i], w_qb[i], kv_norm[i])
            u_pad = jnp.concatenate([jnp.zeros((bsz, CONV_W - 1, D_CONV), u.dtype), u], 1)
            attn = mla_prompt(qn, qp, ckv, kpe, w_uk[i], w_uv[i])
            xp = even_out(xp, short_conv(u_pad, conv_w[i]), gb, za, attn, zb, w_out0[i], ln0_g[i], ln0_b[i])
            conv_p.append(u_pad[:, -(CONV_W - 1):])
            ckv_p.append(ckv)
            kpe_p.append(kpe)
            u, gb, za, qn, qp, ckv, kpe, zb = even_in(xs, pos_s, w_in0[i], q_norm[i], w_qb[i], kv_norm[i])
            u_pad = jnp.concatenate([state_conv[i], u], 1)
            ckv_past = cache_ckv[i, page_table].reshape(dbsz, p_len, KV_RANK)
            kpe_past = cache_kpe[i, page_table].reshape(dbsz, p_len, ROPE_DIM)
            attn = mla_sample(qn, qp, ckv, kpe, ckv_past, kpe_past, w_uk[i], w_uv[i])
            xs = even_out(xs, short_conv(u_pad, conv_w[i]), gb, za, attn, zb, w_out0[i], ln0_g[i], ln0_b[i])
            conv_s.append(u_pad[:, -(CONV_W - 1):])
            ckv_s.append(ckv)
            kpe_s.append(kpe)
        else:
            q, k, v, z = odd_in(xp, w_in1[i])
            xp = odd_out(xp, sb_prompt(q, k, v), z, w_out1[i], ln1_g[i], ln1_b[i])
            k_p.append(k)
            v_p.append(v)
            q, k, v, z = odd_in(xs, w_in1[i])
            k_past = cache_k[i, page_table].reshape(dbsz, p_len, N_KV_SB, HD_SB)
            v_past = cache_v[i, page_table].reshape(dbsz, p_len, N_KV_SB, HD_SB)
            xs = odd_out(xs, sb_sample(q, k, v, k_past, v_past), z, w_out1[i], ln1_g[i], ln1_b[i])
            k_s.append(k)
            v_s.append(v)
    return (xp, xs, jnp.stack(conv_p), jnp.stack(conv_s), jnp.stack(ckv_p), jnp.stack(ckv_s),
            jnp.stack(kpe_p), jnp.stack(kpe_s), jnp.stack(k_p), jnp.stack(k_s),
            jnp.stack(v_p), jnp.stack(v_s))
```

```python
import functools
import math

import jax
import jax.numpy as jnp
from jax import lax
from jax.experimental import pallas as pl
from jax.experimental.pallas import tpu as pltpu

F32 = jnp.float32
BF16 = jnp.bfloat16

D_MODEL = 1024
D_CONV = 512
CONV_W = 3
H_MLA = 8
Q_RANK = 512
KV_RANK = 256
NOPE_DIM = 64
ROPE_DIM = 32
V_DIM = 64
ROPE_THETA = 10000.0
MLA_SCALE = (NOPE_DIM + ROPE_DIM) ** -0.5
H_SB = 16
N_KV_SB = 4
HD_SB = 64
G_SB = H_SB // N_KV_SB
SB_SCALE = HD_SB ** -0.5
MIX1 = H_SB * HD_SB
PAGE_SIZE = 128
DEPTH = 2
DEEPNORM_ALPHA = (2 * DEPTH) ** 0.25
EPS = 1e-5

LANES = 128
SUBLANES = 8
HEAD_PAD = LANES
VMEM_LIMIT = 56 * 1024 * 1024

SB_DEAD_LOG = -104.0

A_H, A_GC, A_GB, A_ZA, A_ZB, A_QA, A_KVA, A_KPE, A_KPES, A_END = (
    0, 512, 1024, 1536, 2048, 2560, 3072, 3328, 3456, 3584)


def _silu(x):
    return x * jax.nn.sigmoid(x)


def _dot(a, b):
    return jnp.dot(a, b, preferred_element_type=F32)


def _dot_nt(a, b):
    return lax.dot_general(a, b, (((1,), (1,)), ((), ())), preferred_element_type=F32)


def _rms(x, g):
    return x * lax.rsqrt(jnp.mean(jnp.square(x), axis=-1, keepdims=True) + EPS) * g


def _layer_norm(x, g, b):
    mu = jnp.mean(x, axis=-1, keepdims=True)
    xc = x - mu
    var = jnp.mean(jnp.square(xc), axis=-1, keepdims=True)
    return xc * lax.rsqrt(var + EPS) * g + b


def _even_in_common(xb, wa_ref, wq_ref, wqs_ref, qn_ref, kvn_ref, tc, ts, tcq, tsq,
                    q_ref, ckv_ref, kpe_ref):
    def proj(a, b):
        return _dot(xb, wa_ref[:, a:b])

    u = proj(A_GC, A_GB) * proj(A_H, A_GC)
    qn = _rms(proj(A_QA, A_KVA), qn_ref[...]).astype(BF16)
    qf = _dot(qn, wq_ref[...])
    qs = _dot(qn, wqs_ref[...])
    for h in range(H_MLA):
        sl = slice(h * HEAD_PAD, (h + 1) * HEAD_PAD)
        q_ref[:, sl] = (qf[:, sl] * tcq + qs[:, sl] * tsq).astype(q_ref.dtype)
    ckv = _rms(proj(A_KVA, A_KPE), kvn_ref[...])
    ckv_ref[...] = ckv
    kpe128 = proj(A_KPE, A_KPES) * tc + proj(A_KPES, A_END) * ts
    kpe_ref[...] = kpe128[:, :ROPE_DIM]
    return u, proj(A_GB, A_ZA), proj(A_ZA, A_ZB), proj(A_ZB, A_QA), ckv, kpe128


def _even_in_prompt_kernel(x_ref, wa_ref, wq_ref, wqs_ref, wkv_ref, qn_ref, kvn_ref, cw_ref,
                           tc_ref, ts_ref, tcq_ref, tsq_ref,
                           ya_ref, szb_ref, q_ref, k_ref, v_ref, ckv_ref, kpe_ref, tail_ref,
                           ubuf, *, tm):
    i = pl.program_id(1)
    xb = x_ref[...].astype(BF16)
    u, gb, za, zb, ckv, kpe128 = _even_in_common(
        xb, wa_ref, wq_ref, wqs_ref, qn_ref, kvn_ref, tc_ref[...], ts_ref[...],
        tcq_ref[...], tsq_ref[...], q_ref, ckv_ref, kpe_ref)

    @pl.when(i == 0)
    def _():
        ubuf[0:SUBLANES, :] = jnp.zeros((SUBLANES, D_CONV), F32)

    @pl.when(i > 0)
    def _():
        ubuf[0:SUBLANES, :] = ubuf[tm:tm + SUBLANES, :]

    ubuf[SUBLANES:tm + SUBLANES, :] = u
    cw = cw_ref[...]
    conv = (cw[0:1, :] * ubuf[SUBLANES - 2:tm + SUBLANES - 2, :]
            + cw[1:2, :] * ubuf[SUBLANES - 1:tm + SUBLANES - 1, :]
            + cw[2:3, :] * u)
    ya_ref[...] = (gb * conv * _silu(za)).astype(ya_ref.dtype)
    szb_ref[...] = _silu(zb)
    tail_ref[...] = ubuf[tm:tm + SUBLANES, :]

    ckvb = ckv.astype(BF16)
    kn = _dot(ckvb, wkv_ref[:, :H_MLA * HEAD_PAD])
    v_ref[...] = _dot(ckvb, wkv_ref[:, H_MLA * HEAD_PAD:]).astype(v_ref.dtype)
    for h in range(H_MLA):
        sl = slice(h * HEAD_PAD, (h + 1) * HEAD_PAD)
        k_ref[:, sl] = (kn[:, sl] + kpe128).astype(k_ref.dtype)


def _even_in_sample_kernel(x_ref, wa_ref, wq_ref, wqs_ref, qn_ref, kvn_ref, cw_ref,
                           tc_ref, ts_ref, tcq_ref, tsq_ref, s0_ref, s1_ref,
                           ya_ref, szb_ref, q_ref, ckv_ref, kpe_ref, u_ref):
    xb = x_ref[...].astype(BF16)
    u, gb, za, zb, _, _ = _even_in_common(
        xb, wa_ref, wq_ref, wqs_ref, qn_ref, kvn_ref, tc_ref[...], ts_ref[...],
        tcq_ref[...], tsq_ref[...], q_ref, ckv_ref, kpe_ref)
    cw = cw_ref[...]
    conv = cw[0:1, :] * s0_ref[...] + cw[1:2, :] * s1_ref[...] + cw[2:3, :] * u
    ya_ref[...] = (gb * conv * _silu(za)).astype(ya_ref.dtype)
    szb_ref[...] = _silu(zb)
    u_ref[...] = u


def _full(shape):
    nd = len(shape)
    return pl.BlockSpec(shape, lambda *_: (0,) * nd)


def _even_in_prompt(x, wa, wq, wqs, wkv, qn, kvn, cw, tc, ts, tcq, tsq):
    bsz, s, _ = x.shape
    tm = min(512, s)
    nt = s // tm
    row = lambda w: pl.BlockSpec((None, tm, w), lambda b, i: (b, i, 0))
    tab = pl.BlockSpec((tm, LANES), lambda b, i: (i, 0))
    outs = (
        jax.ShapeDtypeStruct((bsz, s, D_CONV), BF16),
        jax.ShapeDtypeStruct((bsz, s, H_MLA * V_DIM), F32),
        jax.ShapeDtypeStruct((bsz, s, H_MLA * HEAD_PAD), BF16),
        jax.ShapeDtypeStruct((bsz, s, H_MLA * HEAD_PAD), BF16),
        jax.ShapeDtypeStruct((bsz, s, H_MLA * V_DIM), BF16),
        jax.ShapeDtypeStruct((bsz, s, KV_RANK), F32),
        jax.ShapeDtypeStruct((bsz, s, ROPE_DIM), F32),
        jax.ShapeDtypeStruct((bsz, SUBLANES, D_CONV), F32),
    )
    return pl.pallas_call(
        functools.partial(_even_in_prompt_kernel, tm=tm),
        grid=(bsz, nt),
        in_specs=[row(D_MODEL), _full(wa.shape), _full(wq.shape), _full(wqs.shape), _full(wkv.shape),
                  _full(qn.shape), _full(kvn.shape), _full(cw.shape), tab, tab, tab, tab],
        out_specs=[row(D_CONV), row(H_MLA * V_DIM), row(H_MLA * HEAD_PAD), row(H_MLA * HEAD_PAD),
                   row(H_MLA * V_DIM), row(KV_RANK), row(ROPE_DIM),
                   pl.BlockSpec((None, SUBLANES, D_CONV), lambda b, i: (b, 0, 0))],
        out_shape=outs,
        scratch_shapes=[pltpu.VMEM((tm + SUBLANES, D_CONV), F32)],
        compiler_params=pltpu.CompilerParams(
            dimension_semantics=("arbitrary", "arbitrary"), vmem_limit_bytes=VMEM_LIMIT),
        name="even_in_prompt",
    )(x, wa, wq, wqs, wkv, qn, kvn, cw, tc, ts, tcq, tsq)


def _even_in_sample(x, wa, wq, wqs, qn, kvn, cw, tc, ts, tcq, tsq, s0, s1):
    n = x.shape[0]
    outs = (
        jax.ShapeDtypeStruct((n, D_CONV), BF16),
        jax.ShapeDtypeStruct((n, H_MLA * V_DIM), F32),
        jax.ShapeDtypeStruct((n, H_MLA * HEAD_PAD), F32),
        jax.ShapeDtypeStruct((n, KV_RANK), F32),
        jax.ShapeDtypeStruct((n, ROPE_DIM), F32),
        jax.ShapeDtypeStruct((n, D_CONV), F32),
    )
    args = (x, wa, wq, wqs, qn, kvn, cw, tc, ts, tcq, tsq, s0, s1)
    return pl.pallas_call(
        _even_in_sample_kernel,
        grid=(1,),
        in_specs=[_full(a.shape) for a in args],
        out_specs=[_full(o.shape) for o in outs],
        out_shape=outs,
        compiler_params=pltpu.CompilerParams(vmem_limit_bytes=VMEM_LIMIT),
        name="even_in_sample",
    )(*args)


def _mla_prompt_kernel(q_ref, k_ref, v_ref, o_ref, *, tq, tk):
    qi = pl.program_id(2)
    q = q_ref[...]
    rows = lax.broadcasted_iota(jnp.int32, (tq, tk), 0)
    cols = lax.broadcasted_iota(jnp.int32, (tq, tk), 1)
    diag_mask = cols <= rows

    def block(j, carry, masked):
        kblk = k_ref[pl.ds(pl.multiple_of(j * tk, tk), tk), :]
        vblk = v_ref[pl.ds(pl.multiple_of(j * tk, tk), tk), :]
        new = []
        for e in range(2):
            m, l, acc = carry[e]
            s = _dot_nt(q[:, e * HEAD_PAD:(e + 1) * HEAD_PAD], kblk[:, e * HEAD_PAD:(e + 1) * HEAD_PAD])
            if masked:
                s = jnp.where(diag_mask, s, -jnp.inf)
            m_new = jnp.maximum(m, jnp.max(s, axis=-1, keepdims=True))
            p = jnp.exp(s - m_new)
            alpha = jnp.exp(m - m_new)
            l = alpha * l + jnp.sum(p, axis=-1, keepdims=True)
            acc = alpha * acc + _dot(p.astype(BF16), vblk)
            new.append((m_new, l, acc))
        return tuple(new)

    init = tuple((jnp.full((tq, 1), -jnp.inf, F32), jnp.zeros((tq, 1), F32),
                  jnp.zeros((tq, 2 * V_DIM), F32)) for _ in range(2))
    carry = lax.fori_loop(0, qi, lambda j, c: block(j, c, False), init)
    carry = block(qi, carry, True)
    lane = lax.broadcasted_iota(jnp.int32, (tq, 2 * V_DIM), 1)
    o0 = carry[0][2] / carry[0][1]
    o1 = carry[1][2] / carry[1][1]
    o_ref[...] = jnp.where(lane < V_DIM, o0, o1)


def _mla_prompt(q, k, v):
    bsz, s, _ = q.shape
    tq = tk = min(512, s)
    nq = s // tq
    return pl.pallas_call(
        functools.partial(_mla_prompt_kernel, tq=tq, tk=tk),
        grid=(bsz, H_MLA // 2, nq),
        in_specs=[pl.BlockSpec((None, tq, 2 * HEAD_PAD), lambda b, h, i: (b, i, h)),
                  pl.BlockSpec((None, s, 2 * HEAD_PAD), lambda b, h, i: (b, 0, h)),
                  pl.BlockSpec((None, s, 2 * V_DIM), lambda b, h, i: (b, 0, h))],
        out_specs=pl.BlockSpec((None, tq, 2 * V_DIM), lambda b, h, i: (b, i, h)),
        out_shape=jax.ShapeDtypeStruct((bsz, s, H_MLA * V_DIM), F32),
        compiler_params=pltpu.CompilerParams(
            dimension_semantics=("arbitrary", "arbitrary", "arbitrary"), vmem_limit_bytes=VMEM_LIMIT),
        name="mla_prompt",
    )(q, k, v)


def _mla_sample_kernel(pt_ref, q_ref, wukt_ref, wuv_ref, ckvn_ref, kpen_ref, ckv_ref, kpe_ref,
                       o_ref, qlat, qpe, m_s, l_s, acc):
    del pt_ref
    p_idx = pl.program_id(1)
    n_pages = pl.num_programs(1)
    head_of_group = lax.broadcasted_iota(jnp.int32, (H_MLA, H_MLA * HEAD_PAD), 1) // HEAD_PAD
    row8 = lax.broadcasted_iota(jnp.int32, (H_MLA, H_MLA * HEAD_PAD), 0)

    @pl.when(p_idx == 0)
    def _():
        qrow = q_ref[...]
        qbd = jnp.where(head_of_group == row8, jnp.broadcast_to(qrow, row8.shape), 0.0)
        qlat[...] = _dot(qbd.astype(BF16), wukt_ref[...])
        acc_pe = jnp.zeros((H_MLA, HEAD_PAD), F32)
        for h in range(H_MLA):
            sel = lax.broadcasted_iota(jnp.int32, (H_MLA, HEAD_PAD), 0) == h
            acc_pe = jnp.where(sel, jnp.broadcast_to(qrow[:, h * HEAD_PAD:(h + 1) * HEAD_PAD],
                                                     (H_MLA, HEAD_PAD)), acc_pe)
        qpe[...] = acc_pe[:, :ROPE_DIM]
        m_s[...] = jnp.full(m_s.shape, -jnp.inf, F32)
        l_s[...] = jnp.zeros(l_s.shape, F32)
        acc[...] = jnp.zeros(acc.shape, F32)

    ckv_pg = ckv_ref[...].astype(BF16)
    kpe_pg = kpe_ref[...].astype(BF16)
    s = _dot_nt(qlat[...].astype(BF16), ckv_pg) + _dot_nt(qpe[...].astype(BF16), kpe_pg)
    m_old = m_s[...]
    m_new = jnp.maximum(m_old, jnp.max(s, axis=-1, keepdims=True))
    p = jnp.exp(s - m_new)
    alpha = jnp.exp(m_old - m_new)
    l_s[...] = alpha * l_s[...] + jnp.sum(p, axis=-1, keepdims=True)
    acc[...] = alpha * acc[...] + _dot(p.astype(BF16), ckv_pg)
    m_s[...] = m_new

    @pl.when(p_idx == n_pages - 1)
    def _():
        ckv_new = ckvn_ref[...]
        kpe_new = kpen_ref[...]
        s_new = (jnp.sum(qlat[...] * ckv_new, axis=-1, keepdims=True)
                 + jnp.sum(qpe[...] * kpe_new, axis=-1, keepdims=True))
        m_old = m_s[...]
        m_new = jnp.maximum(m_old, s_new)
        p_new = jnp.exp(s_new - m_new)
        alpha = jnp.exp(m_old - m_new)
        l = alpha * l_s[...] + p_new
        o_lat = (alpha * acc[...] + p_new * ckv_new) / l
        full = _dot(o_lat.astype(BF16), wuv_ref[...])
        hv = lax.broadcasted_iota(jnp.int32, full.shape, 1) // V_DIM
        rv = lax.broadcasted_iota(jnp.int32, full.shape, 0)
        o_ref[...] = jnp.sum(jnp.where(hv == rv, full, 0.0), axis=0, keepdims=True)


def _mla_sample(page_table, q, wukt, wuv, ckv_new, kpe_new, cache_ckv, cache_kpe):
    n, n_pages = page_table.shape
    q3 = q.reshape(n, 1, H_MLA * HEAD_PAD)
    ckv3 = ckv_new.reshape(n, 1, KV_RANK)
    kpe3 = kpe_new.reshape(n, 1, ROPE_DIM)
    grid_spec = pltpu.PrefetchScalarGridSpec(
        num_scalar_prefetch=1,
        grid=(n, n_pages),
        in_specs=[
            pl.BlockSpec((None, 1, H_MLA * HEAD_PAD), lambda b, p, pt: (b, 0, 0)),
            pl.BlockSpec(wukt.shape, lambda b, p, pt: (0, 0)),
            pl.BlockSpec(wuv.shape, lambda b, p, pt: (0, 0)),
            pl.BlockSpec((None, 1, KV_RANK), lambda b, p, pt: (b, 0, 0)),
            pl.BlockSpec((None, 1, ROPE_DIM), lambda b, p, pt: (b, 0, 0)),
            pl.BlockSpec((None, PAGE_SIZE, KV_RANK), lambda b, p, pt: (pt[b, p], 0, 0)),
            pl.BlockSpec((None, PAGE_SIZE, ROPE_DIM), lambda b, p, pt: (pt[b, p], 0, 0)),
        ],
        out_specs=pl.BlockSpec((None, 1, H_MLA * V_DIM), lambda b, p, pt: (b, 0, 0)),
        scratch_shapes=[pltpu.VMEM((H_MLA, KV_RANK), F32), pltpu.VMEM((H_MLA, ROPE_DIM), F32),
                        pltpu.VMEM((H_MLA, 1), F32), pltpu.VMEM((H_MLA, 1), F32),
                        pltpu.VMEM((H_MLA, KV_RANK), F32)],
    )
    out = pl.pallas_call(
        _mla_sample_kernel,
        grid_spec=grid_spec,
        out_shape=jax.ShapeDtypeStruct((n, 1, H_MLA * V_DIM), F32),
        compiler_params=pltpu.CompilerParams(dimension_semantics=("arbitrary", "arbitrary")),
        name="mla_sample",
    )(page_table, q3, wukt, wuv, ckv3, kpe3, cache_ckv, cache_kpe)
    return out.reshape(n, H_MLA * V_DIM)


def _mid_kernel(x_ref, ya_ref, attn_ref, szb_ref, wo_ref, g_ref, b_ref, w1_ref,
                x1_ref, q_ref, k_ref, v_ref, sz_ref, *extra_refs):
    yb = (attn_ref[...] * szb_ref[...]).astype(BF16)
    y = _dot(ya_ref[...], wo_ref[:D_CONV, :]) + _dot(yb, wo_ref[D_CONV:, :])
    x1 = _layer_norm(DEEPNORM_ALPHA * x_ref[...] + y, g_ref[...], b_ref[...])
    x1_ref[...] = x1
    xb = x1.astype(BF16)
    kv_w = N_KV_SB * HD_SB
    c = 0
    q_ref[...] = (_dot(xb, w1_ref[:, c:c + MIX1]) * SB_SCALE).astype(q_ref.dtype)
    c += MIX1
    k_ref[...] = _dot(xb, w1_ref[:, c:c + kv_w])
    c += kv_w
    v_ref[...] = _dot(xb, w1_ref[:, c:c + kv_w])
    c += kv_w
    sz_ref[...] = _silu(_dot(xb, w1_ref[:, c:c + MIX1]))
    c += MIX1
    for r in extra_refs:
        w = r.shape[-1]
        r[...] = _dot(xb, w1_ref[:, c:c + w]).astype(r.dtype)
        c += w


def _mid(x, ya, attn, szb, wo, g, b, w1, n_extra, q_dtype):
    n = x.shape[0]
    tm = min(512, n)
    kv_w = N_KV_SB * HD_SB
    row = lambda w: pl.BlockSpec((tm, w), lambda i: (i, 0))
    outs = [jax.ShapeDtypeStruct((n, D_MODEL), F32), jax.ShapeDtypeStruct((n, MIX1), q_dtype),
            jax.ShapeDtypeStruct((n, kv_w), F32), jax.ShapeDtypeStruct((n, kv_w), F32),
            jax.ShapeDtypeStruct((n, MIX1), F32)]
    outs += [jax.ShapeDtypeStruct((n, N_KV_SB * LANES), BF16)] * n_extra
    return pl.pallas_call(
        _mid_kernel,
        grid=(n // tm,),
        in_specs=[row(D_MODEL), row(D_CONV), row(H_MLA * V_DIM), row(H_MLA * V_DIM),
                  _full(wo.shape), _full(g.shape), _full(b.shape), _full(w1.shape)],
        out_specs=[row(o.shape[1]) for o in outs],
        out_shape=outs,
        compiler_params=pltpu.CompilerParams(
            dimension_semantics=("arbitrary",), vmem_limit_bytes=VMEM_LIMIT),
        name="even_out_odd_in",
    )(x, ya, attn, szb, wo, g, b, w1)


def _sb_logs(z):
    nz = -z
    sp = jnp.log1p(jnp.exp(jnp.minimum(z, nz)))
    return jnp.minimum(z, 0.0) - sp, jnp.minimum(nz, 0.0) - sp


def _suffix_sum_exclusive(x, upper):
    hi = x.astype(BF16)
    lo = (x - hi.astype(F32)).astype(BF16)
    return _dot(hi, upper) + _dot(lo, upper)


def _sb_prompt_kernel(q_ref, ka_ref, kb_ref, v_ref, o_ref, *, tq):
    qi = pl.program_id(2)
    r = lax.broadcasted_iota(jnp.int32, (tq, tq), 0)
    c = lax.broadcasted_iota(jnp.int32, (tq, tq), 1)
    upper = jnp.where(r > c, 1.0, 0.0).astype(BF16)
    diag_mask = c < r
    lane = lax.broadcasted_iota(jnp.int32, (tq, LANES), 1)
    k_refs = (ka_ref, kb_ref)

    def block(qp, j, state, masked):
        start = pl.multiple_of(j * tq, tq)
        vblk = v_ref[pl.ds(start, tq), :]
        new = []
        for e in range(2):
            carry, acc = state[e]
            z = _dot_nt(qp, k_refs[e][pl.ds(start, tq), :])
            lb, lk = _sb_logs(z)
            if masked:
                lk = jnp.where(diag_mask, lk, 0.0)
            later = _suffix_sum_exclusive(lk, upper)
            a = jnp.exp(lb + later + carry)
            if masked:
                a = jnp.where(diag_mask, a, 0.0)
            acc = acc + _dot(a.astype(BF16), vblk)
            carry = carry + later[:, 0:1] + lk[:, 0:1]
            new.append((carry, acc))
        return tuple(new)

    for pr in range(G_SB // 2):
        qp = q_ref[:, pr * LANES:(pr + 1) * LANES]
        init = tuple((jnp.zeros((tq, 1), F32), jnp.zeros((tq, LANES), F32)) for _ in range(2))
        state = block(qp, qi, init, True)

        def alive(st):
            return jnp.max(jnp.maximum(st[0][0], st[1][0])) > SB_DEAD_LOG

        def cond(ls):
            j, st, go = ls
            return jnp.logical_and(j >= 0, go)

        def body(ls):
            j, st, _ = ls
            st = block(qp, j, st, False)
            return j - 1, st, alive(st)

        _, state, _ = lax.while_loop(cond, body, (qi - 1, state, alive(state)))
        o_ref[:, pr * LANES:(pr + 1) * LANES] = jnp.where(lane < HD_SB, state[0][1], state[1][1])


def _sb_prompt(q, ka, kb, vdup):
    bsz, s, _ = q.shape
    tq = min(256, s)
    nq = s // tq
    kvspec = pl.BlockSpec((None, s, LANES), lambda b, j, i: (b, 0, j))
    return pl.pallas_call(
        functools.partial(_sb_prompt_kernel, tq=tq),
        grid=(bsz, N_KV_SB, nq),
        in_specs=[pl.BlockSpec((None, tq, G_SB * HD_SB), lambda b, j, i: (b, i, j)),
                  kvspec, kvspec, kvspec],
        out_specs=pl.BlockSpec((None, tq, G_SB * HD_SB), lambda b, j, i: (b, i, j)),
        out_shape=jax.ShapeDtypeStruct((bsz, s, MIX1), F32),
        compiler_params=pltpu.CompilerParams(
            dimension_semantics=("arbitrary", "arbitrary", "arbitrary"), vmem_limit_bytes=VMEM_LIMIT),
        name="sb_prompt",
    )(q, ka, kb, vdup)


def _sb_sample_kernel(pt_ref, q_ref, k_ref, v_ref, o_ref, qbd, carry, acc):
    del pt_ref
    p_idx = pl.program_id(1)
    n_pages = pl.num_programs(1)
    kv_w = N_KV_SB * HD_SB
    kv_of_lane = lax.broadcasted_iota(jnp.int32, (H_SB, kv_w), 1) // HD_SB
    kv_of_row = lax.broadcasted_iota(jnp.int32, (H_SB, kv_w), 0) % N_KV_SB
    own = kv_of_lane == kv_of_row

    @pl.when(p_idx == 0)
    def _():
        qrow = q_ref[...]
        parts = [jnp.broadcast_to(qrow[:, g * kv_w:(g + 1) * kv_w], (N_KV_SB, kv_w)) for g in range(G_SB)]
        qbd[...] = jnp.where(own, jnp.concatenate(parts, axis=0), 0.0)
        carry[...] = jnp.zeros(carry.shape, F32)
        acc[...] = jnp.zeros(acc.shape, F32)

    r = lax.broadcasted_iota(jnp.int32, (PAGE_SIZE, PAGE_SIZE), 0)
    c = lax.broadcasted_iota(jnp.int32, (PAGE_SIZE, PAGE_SIZE), 1)
    upper = jnp.where(r > c, 1.0, 0.0).astype(BF16)
    z = _dot_nt(qbd[...].astype(BF16), k_ref[...].astype(BF16))
    lb, lk = _sb_logs(z)
    later = _suffix_sum_exclusive(lk, upper)
    a = jnp.exp(lb + later + carry[...])
    acc[...] = acc[...] + _dot(a.astype(BF16), v_ref[...].astype(BF16))
    carry[...] = carry[...] + later[:, 0:1] + lk[:, 0:1]

    @pl.when(p_idx == n_pages - 1)
    def _():
        masked = jnp.where(own, acc[...], 0.0)
        for g in range(G_SB):
            o_ref[:, g * kv_w:(g + 1) * kv_w] = jnp.sum(
                masked[g * N_KV_SB:(g + 1) * N_KV_SB, :], axis=0, keepdims=True)


def _sb_sample(page_table, q, cache_k, cache_v):
    n, n_pages = page_table.shape
    kv_w = N_KV_SB * HD_SB
    page = pl.BlockSpec((None, PAGE_SIZE, kv_w), lambda b, p, pt: (pt[b, n_pages - 1 - p], 0, 0))
    grid_spec = pltpu.PrefetchScalarGridSpec(
        num_scalar_prefetch=1,
        grid=(n, n_pages),
        in_specs=[pl.BlockSpec((None, 1, MIX1), lambda b, p, pt: (b, 0, 0)), page, page],
        out_specs=pl.BlockSpec((None, 1, MIX1), lambda b, p, pt: (b, 0, 0)),
        scratch_shapes=[pltpu.VMEM((H_SB, kv_w), F32), pltpu.VMEM((H_SB, 1), F32),
                        pltpu.VMEM((H_SB, kv_w), F32)],
    )
    out = pl.pallas_call(
        _sb_sample_kernel,
        grid_spec=grid_spec,
        out_shape=jax.ShapeDtypeStruct((n, 1, MIX1), F32),
        compiler_params=pltpu.CompilerParams(dimension_semantics=("arbitrary", "arbitrary")),
        name="sb_sample",
    )(page_table, q.reshape(n, 1, MIX1), cache_k, cache_v)
    return out.reshape(n, MIX1)


def _odd_out_kernel(x_ref, o_ref, sz_ref, wo_ref, g_ref, b_ref, y_ref):
    y = _dot((o_ref[...] * sz_ref[...]).astype(BF16), wo_ref[...])
    y_ref[...] = _layer_norm(DEEPNORM_ALPHA * x_ref[...] + y, g_ref[...], b_ref[...])


def _odd_out(x, o, sz, wo, g, b):
    n = x.shape[0]
    tm = min(512, n)
    row = pl.BlockSpec((tm, D_MODEL), lambda i: (i, 0))
    return pl.pallas_call(
        _odd_out_kernel,
        grid=(n // tm,),
        in_specs=[row, row, row, _full(wo.shape), _full(g.shape), _full(b.shape)],
        out_specs=row,
        out_shape=jax.ShapeDtypeStruct((n, D_MODEL), F32),
        compiler_params=pltpu.CompilerParams(
            dimension_semantics=("arbitrary",), vmem_limit_bytes=VMEM_LIMIT),
        name="odd_out",
    )(x, o, sz, wo, g, b)


def _swap_halves(w):
    half = w.shape[-1] // 2
    return jnp.concatenate([w[..., half:], w[..., :half]], axis=-1)


def _layout_w_in0(w):
    h, gc, gb, za, qa, kva, kpe, zb = jnp.split(
        w, [512, 1024, 1536, 2048, 2560, 2816, 2848], axis=1)
    pad = jnp.zeros((w.shape[0], HEAD_PAD - ROPE_DIM), w.dtype)
    cols = [h, gc, gb, za, zb, qa, kva, kpe, pad, _swap_halves(kpe), pad]
    return jnp.concatenate(cols, axis=1).astype(BF16)


def _layout_w_qb(w):
    w = w.reshape(Q_RANK, H_MLA, NOPE_DIM + ROPE_DIM)
    nope, pe = w[..., :NOPE_DIM], w[..., NOPE_DIM:]
    z32 = jnp.zeros((Q_RANK, H_MLA, HEAD_PAD - NOPE_DIM - ROPE_DIM), w.dtype)
    z96 = jnp.zeros((Q_RANK, H_MLA, HEAD_PAD - ROPE_DIM), w.dtype)
    wq = jnp.concatenate([pe, nope, z32], axis=-1).reshape(Q_RANK, H_MLA * HEAD_PAD)
    wqs = jnp.concatenate([_swap_halves(pe), z96], axis=-1).reshape(Q_RANK, H_MLA * HEAD_PAD)
    return wq.astype(BF16), wqs.astype(BF16)


def _layout_w_uk(w_uk):
    zl = jnp.zeros((KV_RANK, H_MLA, ROPE_DIM), w_uk.dtype)
    zr = jnp.zeros((KV_RANK, H_MLA, HEAD_PAD - NOPE_DIM - ROPE_DIM), w_uk.dtype)
    return jnp.concatenate([zl, w_uk, zr], axis=-1).reshape(KV_RANK, H_MLA * HEAD_PAD)


def _rope_tables(pos):
    half = ROPE_DIM // 2
    freq = ROPE_THETA ** (-jnp.arange(half, dtype=F32) / half)
    ang = pos.astype(F32)[:, None] * freq[None, :]
    cos, sin = jnp.cos(ang), jnp.sin(ang)
    n = pos.shape[0]
    ones = jnp.ones((n, NOPE_DIM), F32)
    zc = jnp.zeros((n, HEAD_PAD - NOPE_DIM - ROPE_DIM), F32)
    zs = jnp.zeros((n, HEAD_PAD - ROPE_DIM), F32)
    tc = jnp.concatenate([cos, cos, ones, zc], axis=1)
    ts = jnp.concatenate([-sin, sin, zs], axis=1)
    return tc, ts


def _layout_w_in1_prompt(w):
    kv_w = N_KV_SB * HD_SB
    q, k, v, z = jnp.split(w, [MIX1, MIX1 + kv_w, MIX1 + 2 * kv_w], axis=1)
    k3 = k.reshape(D_MODEL, N_KV_SB, HD_SB)
    v3 = v.reshape(D_MODEL, N_KV_SB, HD_SB)
    zeros = jnp.zeros_like(k3)
    ka = jnp.concatenate([k3, zeros], axis=-1).reshape(D_MODEL, N_KV_SB * LANES)
    kb = jnp.concatenate([zeros, k3], axis=-1).reshape(D_MODEL, N_KV_SB * LANES)
    vd = jnp.concatenate([v3, v3], axis=-1).reshape(D_MODEL, N_KV_SB * LANES)
    return jnp.concatenate([q, k, v, z, ka, kb, vd], axis=1).astype(BF16)


def _group_major_cols(w):
    lead = w.shape[0]
    return w.reshape(lead, N_KV_SB, G_SB, HD_SB).transpose(0, 2, 1, 3).reshape(lead, MIX1)


def _layout_w_in1_sample(w):
    kv_w = N_KV_SB * HD_SB
    q, k, v, z = jnp.split(w, [MIX1, MIX1 + kv_w, MIX1 + 2 * kv_w], axis=1)
    return jnp.concatenate([_group_major_cols(q), k, v, _group_major_cols(z)], axis=1).astype(BF16)


def kernel(x_prompt, x_sample, state_conv, cache_ckv, cache_kpe, cache_k, cache_v, page_table,
           w_in0, conv_w, q_norm, w_qb, kv_norm, w_uk, w_uv, w_out0, ln0_g, ln0_b,
           w_in1, w_out1, ln1_g, ln1_b):
    bsz, s_len, _ = x_prompt.shape
    dbsz, t_len, _ = x_sample.shape
    assert t_len == 1, "the sample group is one new token per sequence"
    n_pages = page_table.shape[1]
    p_len = n_pages * PAGE_SIZE
    n_phys = cache_ckv.shape[1]
    kv_w = N_KV_SB * HD_SB
    rows_p = bsz * s_len

    wa = _layout_w_in0(w_in0[0])
    wq, wqs = _layout_w_qb(w_qb[0])
    wuk_pad = _layout_w_uk(w_uk[0])
    wuv = w_uv[0].reshape(KV_RANK, H_MLA * V_DIM)
    wkv = jnp.concatenate([wuk_pad, wuv], axis=1).astype(BF16)
    wukt = wuk_pad.T.astype(BF16)
    wuv_b = wuv.astype(BF16)
    qn = q_norm[0].reshape(1, Q_RANK)
    kvn = kv_norm[0].reshape(1, KV_RANK)
    cw = conv_w[0]
    wo0 = w_out0[0].astype(BF16)
    g0, b0 = ln0_g[0].reshape(1, D_MODEL), ln0_b[0].reshape(1, D_MODEL)
    g1, b1 = ln1_g[0].reshape(1, D_MODEL), ln1_b[0].reshape(1, D_MODEL)
    w1p = _layout_w_in1_prompt(w_in1[0])
    w1s = _layout_w_in1_sample(w_in1[0])
    wo1 = w_out1[0].astype(BF16)
    wo1_gm = _group_major_cols(w_out1[0].T).T.astype(BF16)
    tc_p, ts_p = _rope_tables(jnp.arange(s_len))
    tc_s, ts_s = _rope_tables(jnp.full((dbsz,), p_len))

    ya, szb, q, k, v, ckv_p, kpe_p, tail = _even_in_prompt(
        x_prompt, wa, wq, wqs, wkv, qn, kvn, cw, tc_p, ts_p, tc_p * MLA_SCALE, ts_p * MLA_SCALE)
    attn = _mla_prompt(q, k, v)
    x1, q1, k1, v1, sz1, ka, kb, vd = _mid(
        x_prompt.reshape(rows_p, D_MODEL), ya.reshape(rows_p, D_CONV),
        attn.reshape(rows_p, H_MLA * V_DIM), szb.reshape(rows_p, H_MLA * V_DIM),
        wo0, g0, b0, w1p, 3, BF16)
    o1 = _sb_prompt(q1.reshape(bsz, s_len, MIX1), ka.reshape(bsz, s_len, N_KV_SB * LANES),
                    kb.reshape(bsz, s_len, N_KV_SB * LANES), vd.reshape(bsz, s_len, N_KV_SB * LANES))
    y_prompt = _odd_out(x1, o1.reshape(rows_p, MIX1), sz1, wo1, g1, b1).reshape(bsz, s_len, D_MODEL)

    xs = x_sample.reshape(dbsz, D_MODEL)
    ya_s, szb_s, q_s, ckv_s, kpe_s, u_s = _even_in_sample(
        xs, wa, wq, wqs, qn, kvn, cw, tc_s, ts_s, tc_s * MLA_SCALE, ts_s * MLA_SCALE,
        state_conv[0, :, 0, :], state_conv[0, :, 1, :])
    attn_s = _mla_sample(page_table, q_s, wukt, wuv_b, ckv_s, kpe_s, cache_ckv[0], cache_kpe[0])
    x1_s, q1_s, k1_s, v1_s, sz1_s = _mid(xs, ya_s, attn_s, szb_s, wo0, g0, b0, w1s, 0, F32)
    o1_s = _sb_sample(page_table, q1_s, cache_k[0].reshape(n_phys, PAGE_SIZE, kv_w),
                      cache_v[0].reshape(n_phys, PAGE_SIZE, kv_w))
    y_sample = _odd_out(x1_s, o1_s, sz1_s, wo1_gm, g1, b1).reshape(dbsz, 1, D_MODEL)

    conv_prompt = tail[:, SUBLANES - (CONV_W - 1):, :][None]
    conv_sample = jnp.stack([state_conv[0, :, 1, :], u_s], axis=1)[None]
    return (y_prompt, y_sample, conv_prompt, conv_sample,
            ckv_p[None], ckv_s.reshape(1, dbsz, 1, KV_RANK),
            kpe_p[None], kpe_s.reshape(1, dbsz, 1, ROPE_DIM),
            k1.reshape(1, bsz, s_len, N_KV_SB, HD_SB), k1_s.reshape(1, dbsz, 1, N_KV_SB, HD_SB),
            v1.reshape(1, bsz, s_len, N_KV_SB, HD_SB), v1_s.reshape(1, dbsz, 1, N_KV_SB, HD_SB))
```

```python
import functools
import math

import jax
import jax.numpy as jnp
from jax import lax
from jax.experimental import pallas as pl
from jax.experimental.pallas import tpu as pltpu

F32 = jnp.float32
BF16 = jnp.bfloat16

D_MODEL = 1024
D_CONV = 512
CONV_W = 3
H_MLA = 8
Q_RANK = 512
KV_RANK = 256
NOPE_DIM = 64
ROPE_DIM = 32
V_DIM = 64
ROPE_THETA = 10000.0
MLA_SCALE = (NOPE_DIM + ROPE_DIM) ** -0.5
H_SB = 16
N_KV_SB = 4
HD_SB = 64
G_SB = H_SB // N_KV_SB
SB_SCALE = HD_SB ** -0.5
MIX1 = H_SB * HD_SB
PAGE_SIZE = 128
DEPTH = 2
DEEPNORM_ALPHA = (2 * DEPTH) ** 0.25
EPS = 1e-5

LANES = 128
SUBLANES = 8
HEAD_PAD = LANES
VMEM_LIMIT = 56 * 1024 * 1024

SB_DEAD_LOG = -104.0

LOG2E = math.log2(math.e)

MLA_TQ = 512
MLA_TK = 1024

MLA_PAGES_PER_CHUNK = 16
SB_PAGES_PER_CHUNK = 2

A_H, A_GC, A_GB, A_ZA, A_ZB, A_QA, A_KVA, A_KPE, A_KPES, A_END = (
    0, 512, 1024, 1536, 2048, 2560, 3072, 3328, 3456, 3584)


def _silu(x):
    return x * jax.nn.sigmoid(x)


def _dot(a, b):
    return jnp.dot(a, b, preferred_element_type=F32)


def _dot_nt(a, b):
    return lax.dot_general(a, b, (((1,), (1,)), ((), ())), preferred_element_type=F32)


def _rms(x, g):
    return x * lax.rsqrt(jnp.mean(jnp.square(x), axis=-1, keepdims=True) + EPS) * g


def _layer_norm(x, g, b):
    mu = jnp.mean(x, axis=-1, keepdims=True)
    xc = x - mu
    var = jnp.mean(jnp.square(xc), axis=-1, keepdims=True)
    return xc * lax.rsqrt(var + EPS) * g + b


def _even_in_common(xb, wa_ref, wq_ref, wqs_ref, qn_ref, kvn_ref, tc, ts, tcq, tsq,
                    q_ref, ckv_ref, kpe_ref):
    def proj(a, b):
        return _dot(xb, wa_ref[:, a:b])

    u = proj(A_GC, A_GB) * proj(A_H, A_GC)
    qn = _rms(proj(A_QA, A_KVA), qn_ref[...]).astype(BF16)
    qf = _dot(qn, wq_ref[...])
    qs = _dot(qn, wqs_ref[...])
    for h in range(H_MLA):
        sl = slice(h * HEAD_PAD, (h + 1) * HEAD_PAD)
        q_ref[:, sl] = (qf[:, sl] * tcq + qs[:, sl] * tsq).astype(q_ref.dtype)
    ckv = _rms(proj(A_KVA, A_KPE), kvn_ref[...])
    ckv_ref[...] = ckv
    kpe128 = proj(A_KPE, A_KPES) * tc + proj(A_KPES, A_END) * ts
    kpe_ref[...] = kpe128[:, :ROPE_DIM]
    return u, proj(A_GB, A_ZA), proj(A_ZA, A_ZB), proj(A_ZB, A_QA), ckv, kpe128


def _even_in_prompt_kernel(x_ref, wa_ref, wq_ref, wqs_ref, wkv_ref, qn_ref, kvn_ref, cw_ref,
                           tc_ref, ts_ref, tcq_ref, tsq_ref,
                           ya_ref, szb_ref, q_ref, k_ref, v_ref, ckv_ref, kpe_ref, tail_ref,
                           ubuf, *, tm):
    i = pl.program_id(1)
    xb = x_ref[...].astype(BF16)
    u, gb, za, zb, ckv, kpe128 = _even_in_common(
        xb, wa_ref, wq_ref, wqs_ref, qn_ref, kvn_ref, tc_ref[...], ts_ref[...],
        tcq_ref[...], tsq_ref[...], q_ref, ckv_ref, kpe_ref)

    @pl.when(i == 0)
    def _():
        ubuf[0:SUBLANES, :] = jnp.zeros((SUBLANES, D_CONV), F32)

    @pl.when(i > 0)
    def _():
        ubuf[0:SUBLANES, :] = ubuf[tm:tm + SUBLANES, :]

    ubuf[SUBLANES:tm + SUBLANES, :] = u
    cw = cw_ref[...]
    conv = (cw[0:1, :] * ubuf[SUBLANES - 2:tm + SUBLANES - 2, :]
            + cw[1:2, :] * ubuf[SUBLANES - 1:tm + SUBLANES - 1, :]
            + cw[2:3, :] * u)
    ya_ref[...] = (gb * conv * _silu(za)).astype(ya_ref.dtype)
    szb_ref[...] = _silu(zb)
    tail_ref[...] = ubuf[tm:tm + SUBLANES, :]

    ckvb = ckv.astype(BF16)
    kn = _dot(ckvb, wkv_ref[:, :H_MLA * HEAD_PAD])
    v_ref[...] = _dot(ckvb, wkv_ref[:, H_MLA * HEAD_PAD:]).astype(v_ref.dtype)
    for h in range(H_MLA):
        sl = slice(h * HEAD_PAD, (h + 1) * HEAD_PAD)
        k_ref[:, sl] = (kn[:, sl] + kpe128).astype(k_ref.dtype)


def _even_in_sample_kernel(x_ref, wa_ref, wq_ref, wqs_ref, qn_ref, kvn_ref, cw_ref,
                           tc_ref, ts_ref, tcq_ref, tsq_ref, s0_ref, s1_ref,
                           ya_ref, szb_ref, q_ref, ckv_ref, kpe_ref, u_ref):
    xb = x_ref[...].astype(BF16)
    u, gb, za, zb, _, _ = _even_in_common(
        xb, wa_ref, wq_ref, wqs_ref, qn_ref, kvn_ref, tc_ref[...], ts_ref[...],
        tcq_ref[...], tsq_ref[...], q_ref, ckv_ref, kpe_ref)
    cw = cw_ref[...]
    conv = cw[0:1, :] * s0_ref[...] + cw[1:2, :] * s1_ref[...] + cw[2:3, :] * u
    ya_ref[...] = (gb * conv * _silu(za)).astype(ya_ref.dtype)
    szb_ref[...] = _silu(zb)
    u_ref[...] = u


def _full(shape):
    nd = len(shape)
    return pl.BlockSpec(shape, lambda *_: (0,) * nd)


def _even_in_prompt(x, wa, wq, wqs, wkv, qn, kvn, cw, tc, ts, tcq, tsq):
    bsz, s, _ = x.shape
    tm = min(512, s)
    nt = s // tm
    row = lambda w: pl.BlockSpec((None, tm, w), lambda b, i: (b, i, 0))
    tab = pl.BlockSpec((tm, LANES), lambda b, i: (i, 0))
    outs = (
        jax.ShapeDtypeStruct((bsz, s, D_CONV), BF16),
        jax.ShapeDtypeStruct((bsz, s, H_MLA * V_DIM), F32),
        jax.ShapeDtypeStruct((bsz, s, H_MLA * HEAD_PAD), BF16),
        jax.ShapeDtypeStruct((bsz, s, H_MLA * HEAD_PAD), BF16),
        jax.ShapeDtypeStruct((bsz, s, H_MLA * V_DIM), BF16),
        jax.ShapeDtypeStruct((bsz, s, KV_RANK), F32),
        jax.ShapeDtypeStruct((bsz, s, ROPE_DIM), F32),
        jax.ShapeDtypeStruct((bsz, SUBLANES, D_CONV), F32),
    )
    return pl.pallas_call(
        functools.partial(_even_in_prompt_kernel, tm=tm),
        grid=(bsz, nt),
        in_specs=[row(D_MODEL), _full(wa.shape), _full(wq.shape), _full(wqs.shape), _full(wkv.shape),
                  _full(qn.shape), _full(kvn.shape), _full(cw.shape), tab, tab, tab, tab],
        out_specs=[row(D_CONV), row(H_MLA * V_DIM), row(H_MLA * HEAD_PAD), row(H_MLA * HEAD_PAD),
                   row(H_MLA * V_DIM), row(KV_RANK), row(ROPE_DIM),
                   pl.BlockSpec((None, SUBLANES, D_CONV), lambda b, i: (b, 0, 0))],
        out_shape=outs,
        scratch_shapes=[pltpu.VMEM((tm + SUBLANES, D_CONV), F32)],
        compiler_params=pltpu.CompilerParams(
            dimension_semantics=("arbitrary", "arbitrary"), vmem_limit_bytes=VMEM_LIMIT),
        name="even_in_prompt",
    )(x, wa, wq, wqs, wkv, qn, kvn, cw, tc, ts, tcq, tsq)


def _even_in_sample(x, wa, wq, wqs, qn, kvn, cw, tc, ts, tcq, tsq, s0, s1):
    n = x.shape[0]
    outs = (
        jax.ShapeDtypeStruct((n, D_CONV), BF16),
        jax.ShapeDtypeStruct((n, H_MLA * V_DIM), F32),
        jax.ShapeDtypeStruct((n, H_MLA * HEAD_PAD), F32),
        jax.ShapeDtypeStruct((n, KV_RANK), F32),
        jax.ShapeDtypeStruct((n, ROPE_DIM), F32),
        jax.ShapeDtypeStruct((n, D_CONV), F32),
    )
    args = (x, wa, wq, wqs, qn, kvn, cw, tc, ts, tcq, tsq, s0, s1)
    return pl.pallas_call(
        _even_in_sample_kernel,
        grid=(1,),
        in_specs=[_full(a.shape) for a in args],
        out_specs=[_full(o.shape) for o in outs],
        out_shape=outs,
        compiler_params=pltpu.CompilerParams(vmem_limit_bytes=VMEM_LIMIT),
        name="even_in_sample",
    )(*args)


def _mla_prompt_kernel(q_ref, k_ref, v_ref, o_ref, *, tq, tk):
    qi = pl.program_id(2)
    q = q_ref[...]
    col_minus_row = (lax.broadcasted_iota(jnp.int32, (2 * tq, tk), 1)
                     - lax.rem(lax.broadcasted_iota(jnp.int32, (2 * tq, tk), 0), tq))

    def block(j, carry, masked):
        m, l, acc = carry
        start = pl.multiple_of(j * tk, tk)
        s = jnp.concatenate(
            [_dot_nt(q[:, e * HEAD_PAD:(e + 1) * HEAD_PAD],
                     k_ref[pl.ds(start, tk), e * HEAD_PAD:(e + 1) * HEAD_PAD]) for e in range(2)],
            axis=0)
        if masked:
            s = jnp.where(col_minus_row <= qi * tq - j * tk, s, -jnp.inf)
        m_new = jnp.maximum(m, jnp.max(s, axis=-1, keepdims=True))
        p = jnp.exp2(s - m_new)
        alpha = jnp.exp2(m - m_new)
        l = alpha * l + jnp.sum(p, axis=-1, keepdims=True)
        acc = alpha * acc + _dot(p.astype(BF16), v_ref[pl.ds(start, tk), :])
        return m_new, l, acc

    init = (jnp.full((2 * tq, 1), -jnp.inf, F32), jnp.zeros((2 * tq, 1), F32),
            jnp.zeros((2 * tq, 2 * V_DIM), F32))
    n_full = (qi * tq + 1) // tk
    n_end = ((qi + 1) * tq + tk - 1) // tk
    carry = lax.fori_loop(0, n_full, lambda j, c: block(j, c, False), init)
    _, l, acc = lax.fori_loop(n_full, n_end, lambda j, c: block(j, c, True), carry)
    o = acc / l
    lane = lax.broadcasted_iota(jnp.int32, (tq, 2 * V_DIM), 1)
    o_ref[...] = jnp.where(lane < V_DIM, o[:tq], o[tq:])


def _mla_prompt(q, k, v):
    bsz, s, _ = q.shape
    tq, tk = min(MLA_TQ, s), min(MLA_TK, s)
    nq = s // tq
    return pl.pallas_call(
        functools.partial(_mla_prompt_kernel, tq=tq, tk=tk),
        grid=(bsz, H_MLA // 2, nq),
        in_specs=[pl.BlockSpec((None, tq, 2 * HEAD_PAD), lambda b, h, i: (b, i, h)),
                  pl.BlockSpec((None, s, 2 * HEAD_PAD), lambda b, h, i: (b, 0, h)),
                  pl.BlockSpec((None, s, 2 * V_DIM), lambda b, h, i: (b, 0, h))],
        out_specs=pl.BlockSpec((None, tq, 2 * V_DIM), lambda b, h, i: (b, i, h)),
        out_shape=jax.ShapeDtypeStruct((bsz, s, H_MLA * V_DIM), F32),
        compiler_params=pltpu.CompilerParams(
            dimension_semantics=("arbitrary", "arbitrary", "arbitrary"), vmem_limit_bytes=VMEM_LIMIT),
        name="mla_prompt",
    )(q, k, v)


def _mla_sample_kernel(pt_ref, q_ref, wukt_ref, wuv_ref, ckvn_ref, kpen_ref, ckv_hbm, kpet_hbm,
                       o_ref, ckv_buf, kpe_buf, sems, *, pages_per_chunk, n_chunks):
    b = pl.program_id(0)
    nb = pl.num_programs(0)
    cp = pages_per_chunk

    def chunk_copies(seq, c, slot):
        copies = []
        for i in range(cp):
            page = pt_ref[seq, c * cp + i]
            copies.append(pltpu.make_async_copy(ckv_hbm.at[page], ckv_buf.at[slot, i], sems.at[slot, 0]))
            copies.append(pltpu.make_async_copy(kpet_hbm.at[page], kpe_buf.at[slot, i], sems.at[slot, 1]))
        return copies

    def start(seq, c, slot):
        for copy in chunk_copies(seq, c, slot):
            copy.start()

    def wait(seq, c, slot):
        for copy in chunk_copies(seq, c, slot):
            copy.wait()

    @pl.when(b == 0)
    def _():
        start(0, 0, 0)

    head_of_group = lax.broadcasted_iota(jnp.int32, (H_MLA, H_MLA * HEAD_PAD), 1) // HEAD_PAD
    row8 = lax.broadcasted_iota(jnp.int32, (H_MLA, H_MLA * HEAD_PAD), 0)
    qrow = q_ref[...]
    qbd = jnp.where(head_of_group == row8, jnp.broadcast_to(qrow, row8.shape), 0.0)
    qlat = _dot(qbd.astype(BF16), wukt_ref[...])
    q8 = jnp.zeros((H_MLA, HEAD_PAD), F32)
    sub = lax.broadcasted_iota(jnp.int32, (H_MLA, HEAD_PAD), 0)
    for h in range(H_MLA):
        q8 = jnp.where(sub == h, jnp.broadcast_to(qrow[:, h * HEAD_PAD:(h + 1) * HEAD_PAD],
                                                  (H_MLA, HEAD_PAD)), q8)
    qpe = q8[:, :ROPE_DIM]
    qlat_b = qlat.astype(BF16)
    qpe_b = qpe.astype(BF16)

    def chunk(c, carry):
        m_old, l_old, acc = carry
        slot = (b * n_chunks + c) % 2

        @pl.when(c + 1 < n_chunks)
        def _():
            start(b, c + 1, 1 - slot)

        @pl.when(jnp.logical_and(c + 1 == n_chunks, b + 1 < nb))
        def _():
            start(b + 1, 0, 1 - slot)

        wait(b, c, slot)
        ckv = ckv_buf[slot].reshape(cp * PAGE_SIZE, KV_RANK).astype(BF16)
        kpet = jnp.concatenate([kpe_buf[slot, i] for i in range(cp)], axis=1).astype(BF16)
        s = _dot_nt(qlat_b, ckv) + _dot(qpe_b, kpet)
        m_new = jnp.maximum(m_old, jnp.max(s, axis=-1, keepdims=True))
        p = jnp.exp(s - m_new)
        alpha = jnp.exp(m_old - m_new)
        l_new = alpha * l_old + jnp.sum(p, axis=-1, keepdims=True)
        acc = alpha * acc + _dot(p.astype(BF16), ckv)
        return m_new, l_new, acc

    init = (jnp.full((H_MLA, 1), -jnp.inf, F32), jnp.zeros((H_MLA, 1), F32),
            jnp.zeros((H_MLA, KV_RANK), F32))
    m_old, l_old, acc = lax.fori_loop(0, n_chunks, chunk, init)

    ckv_new = ckvn_ref[...]
    kpe_new = kpen_ref[...]
    s_new = (jnp.sum(qlat * ckv_new, axis=-1, keepdims=True)
             + jnp.sum(qpe * kpe_new, axis=-1, keepdims=True))
    m_new = jnp.maximum(m_old, s_new)
    p_new = jnp.exp(s_new - m_new)
    alpha = jnp.exp(m_old - m_new)
    l = alpha * l_old + p_new
    o_lat = (alpha * acc + p_new * ckv_new) / l
    full = _dot(o_lat.astype(BF16), wuv_ref[...])
    hv = lax.broadcasted_iota(jnp.int32, full.shape, 1) // V_DIM
    rv = lax.broadcasted_iota(jnp.int32, full.shape, 0)
    o_ref[...] = jnp.sum(jnp.where(hv == rv, full, 0.0), axis=0, keepdims=True)


def _mla_sample(page_table, q, wukt, wuv, ckv_new, kpe_new, cache_ckv, cache_kpet):
    n, n_pages = page_table.shape
    q3 = q.reshape(n, 1, H_MLA * HEAD_PAD)
    ckv3 = ckv_new.reshape(n, 1, KV_RANK)
    kpe3 = kpe_new.reshape(n, 1, ROPE_DIM)
    cp = math.gcd(MLA_PAGES_PER_CHUNK, n_pages)
    grid_spec = pltpu.PrefetchScalarGridSpec(
        num_scalar_prefetch=1,
        grid=(n,),
        in_specs=[
            pl.BlockSpec((None, 1, H_MLA * HEAD_PAD), lambda b, pt: (b, 0, 0)),
            pl.BlockSpec(wukt.shape, lambda b, pt: (0, 0)),
            pl.BlockSpec(wuv.shape, lambda b, pt: (0, 0)),
            pl.BlockSpec((None, 1, KV_RANK), lambda b, pt: (b, 0, 0)),
            pl.BlockSpec((None, 1, ROPE_DIM), lambda b, pt: (b, 0, 0)),
            pl.BlockSpec(memory_space=pl.ANY),
            pl.BlockSpec(memory_space=pl.ANY),
        ],
        out_specs=pl.BlockSpec((None, 1, H_MLA * V_DIM), lambda b, pt: (b, 0, 0)),
        scratch_shapes=[pltpu.VMEM((2, cp, PAGE_SIZE, KV_RANK), F32),
                        pltpu.VMEM((2, cp, ROPE_DIM, PAGE_SIZE), F32),
                        pltpu.SemaphoreType.DMA((2, 2))],
    )
    out = pl.pallas_call(
        functools.partial(_mla_sample_kernel, pages_per_chunk=cp, n_chunks=n_pages // cp),
        grid_spec=grid_spec,
        out_shape=jax.ShapeDtypeStruct((n, 1, H_MLA * V_DIM), F32),
        compiler_params=pltpu.CompilerParams(
            dimension_semantics=("arbitrary",), vmem_limit_bytes=VMEM_LIMIT),
        name="mla_sample",
    )(page_table, q3, wukt, wuv, ckv3, kpe3, cache_ckv, cache_kpet)
    return out.reshape(n, H_MLA * V_DIM)


def _mid_kernel(x_ref, ya_ref, attn_ref, szb_ref, wo_ref, g_ref, b_ref, w1_ref,
                x1_ref, q_ref, k_ref, v_ref, sz_ref, *extra_refs):
    yb = (attn_ref[...] * szb_ref[...]).astype(BF16)
    y = _dot(ya_ref[...], wo_ref[:D_CONV, :]) + _dot(yb, wo_ref[D_CONV:, :])
    x1 = _layer_norm(DEEPNORM_ALPHA * x_ref[...] + y, g_ref[...], b_ref[...])
    x1_ref[...] = x1
    xb = x1.astype(BF16)
    kv_w = N_KV_SB * HD_SB
    c = 0
    q_ref[...] = (_dot(xb, w1_ref[:, c:c + MIX1]) * SB_SCALE).astype(q_ref.dtype)
    c += MIX1
    k_ref[...] = _dot(xb, w1_ref[:, c:c + kv_w])
    c += kv_w
    v_ref[...] = _dot(xb, w1_ref[:, c:c + kv_w])
    c += kv_w
    sz_ref[...] = _silu(_dot(xb, w1_ref[:, c:c + MIX1]))
    c += MIX1
    for r in extra_refs:
        w = r.shape[-1]
        r[...] = _dot(xb, w1_ref[:, c:c + w]).astype(r.dtype)
        c += w


def _mid(x, ya, attn, szb, wo, g, b, w1, n_extra, q_dtype):
    n = x.shape[0]
    tm = min(512, n)
    kv_w = N_KV_SB * HD_SB
    row = lambda w: pl.BlockSpec((tm, w), lambda i: (i, 0))
    outs = [jax.ShapeDtypeStruct((n, D_MODEL), F32), jax.ShapeDtypeStruct((n, MIX1), q_dtype),
            jax.ShapeDtypeStruct((n, kv_w), F32), jax.ShapeDtypeStruct((n, kv_w), F32),
            jax.ShapeDtypeStruct((n, MIX1), F32)]
    outs += [jax.ShapeDtypeStruct((n, N_KV_SB * LANES), BF16)] * n_extra
    return pl.pallas_call(
        _mid_kernel,
        grid=(n // tm,),
        in_specs=[row(D_MODEL), row(D_CONV), row(H_MLA * V_DIM), row(H_MLA * V_DIM),
                  _full(wo.shape), _full(g.shape), _full(b.shape), _full(w1.shape)],
        out_specs=[row(o.shape[1]) for o in outs],
        out_shape=outs,
        compiler_params=pltpu.CompilerParams(
            dimension_semantics=("arbitrary",), vmem_limit_bytes=VMEM_LIMIT),
        name="even_out_odd_in",
    )(x, ya, attn, szb, wo, g, b, w1)


def _sb_logs(z):
    sp = jnp.log(1.0 + jnp.exp(-jnp.abs(z)))
    lb = jnp.minimum(z, 0.0) - sp
    return lb, lb - z


def _upper2(n):
    r = lax.broadcasted_iota(jnp.int32, (2 * n, n), 0)
    c = lax.broadcasted_iota(jnp.int32, (2 * n, n), 1)
    return jnp.where(jnp.where(r >= n, r - n, r) > c, 1.0, 0.0).astype(BF16)


def _suffix_sum_exclusive(x, upper2):
    hi = x.astype(BF16)
    lo = (x - hi.astype(F32)).astype(BF16)
    return _dot(jnp.concatenate([hi, lo], axis=1), upper2)


def _sb_prompt_kernel(q_ref, ka_ref, kb_ref, v_ref, o_ref, *, tq):
    qi = pl.program_id(2)
    upper = _upper2(tq)
    q2 = jnp.concatenate([q_ref[:, :LANES], q_ref[:, LANES:]], axis=0)
    rows = G_SB * tq
    row_in_tile = lax.rem(lax.broadcasted_iota(jnp.int32, (rows, tq), 0), tq)
    diag_mask = lax.broadcasted_iota(jnp.int32, (rows, tq), 1) < row_in_tile

    def block(j, state, masked):
        carry, acc = state
        start = pl.multiple_of(j * tq, tq)
        z = jnp.concatenate([_dot_nt(q2, ka_ref[pl.ds(start, tq), :]),
                             _dot_nt(q2, kb_ref[pl.ds(start, tq), :])], axis=0)
        lb, lk = _sb_logs(z)
        if masked:
            lk = jnp.where(diag_mask, lk, 0.0)
        later = _suffix_sum_exclusive(lk, upper)
        a = jnp.exp(lb + later + carry)
        if masked:
            a = jnp.where(diag_mask, a, 0.0)
        acc = acc + _dot(a.astype(BF16), v_ref[pl.ds(start, tq), :])
        carry = carry + later[:, 0:1] + lk[:, 0:1]
        return carry, acc

    def alive(st):
        return jnp.max(st[0]) > SB_DEAD_LOG

    def cond(ls):
        j, st, go = ls
        return jnp.logical_and(j >= 0, go)

    def body(ls):
        j, st, _ = ls
        st = block(j, st, False)
        return j - 1, st, alive(st)

    state = block(qi, (jnp.zeros((rows, 1), F32), jnp.zeros((rows, LANES), F32)), True)
    _, (_, acc), _ = lax.while_loop(cond, body, (qi - 1, state, alive(state)))
    lane = lax.broadcasted_iota(jnp.int32, (tq, LANES), 1)
    o_ref[:, :LANES] = jnp.where(lane < HD_SB, acc[0:tq], acc[2 * tq:3 * tq])
    o_ref[:, LANES:] = jnp.where(lane < HD_SB, acc[tq:2 * tq], acc[3 * tq:4 * tq])


def _sb_prompt(q, ka, kb, vdup):
    bsz, s, _ = q.shape
    tq = min(256, s)
    nq = s // tq
    kvspec = pl.BlockSpec((None, s, LANES), lambda b, j, i: (b, 0, j))
    return pl.pallas_call(
        functools.partial(_sb_prompt_kernel, tq=tq),
        grid=(bsz, N_KV_SB, nq),
        in_specs=[pl.BlockSpec((None, tq, G_SB * HD_SB), lambda b, j, i: (b, i, j)),
                  kvspec, kvspec, kvspec],
        out_specs=pl.BlockSpec((None, tq, G_SB * HD_SB), lambda b, j, i: (b, i, j)),
        out_shape=jax.ShapeDtypeStruct((bsz, s, MIX1), F32),
        compiler_params=pltpu.CompilerParams(
            dimension_semantics=("arbitrary", "arbitrary", "arbitrary"), vmem_limit_bytes=VMEM_LIMIT),
        name="sb_prompt",
    )(q, ka, kb, vdup)


def _sb_sample_kernel(pt_ref, q_ref, kt_hbm, vt_hbm, o_ref, kbuf, vbuf, sems, carry, acc,
                      *, pages_per_chunk, n_pages):
    b = pl.program_id(0)
    nb = pl.num_programs(0)
    cp = pages_per_chunk
    n_chunks = n_pages // cp
    kv_w = N_KV_SB * HD_SB
    keys = cp * PAGE_SIZE

    def chunk_copies(seq, c, slot):
        copies = []
        for i in range(cp):
            page = pt_ref[seq, n_pages - 1 - (c * cp + i)]
            copies.append(pltpu.make_async_copy(kt_hbm.at[page], kbuf.at[slot, i], sems.at[slot, 0]))
            copies.append(pltpu.make_async_copy(vt_hbm.at[page], vbuf.at[slot, i], sems.at[slot, 1]))
        return copies

    def start(seq, c, slot):
        for copy in chunk_copies(seq, c, slot):
            copy.start()

    def wait(seq, c, slot):
        for copy in chunk_copies(seq, c, slot):
            copy.wait()

    first_slot = b % 2

    @pl.when(b == 0)
    def _():
        start(0, 0, 0)

    @pl.when(b + 1 < nb)
    def _():
        start(b + 1, 0, 1 - first_slot)

    kv_of_lane = lax.broadcasted_iota(jnp.int32, (H_SB, kv_w), 1) // HD_SB
    kv_of_row = lax.broadcasted_iota(jnp.int32, (H_SB, kv_w), 0) % N_KV_SB
    own = kv_of_lane == kv_of_row
    qrow = q_ref[...]
    parts = [jnp.broadcast_to(qrow[:, g * kv_w:(g + 1) * kv_w], (N_KV_SB, kv_w)) for g in range(G_SB)]
    qbd = jnp.where(own, jnp.concatenate(parts, axis=0), 0.0).astype(BF16)
    upper = _upper2(keys)
    carry[...] = jnp.zeros(carry.shape, F32)
    acc[...] = jnp.zeros(acc.shape, F32)

    def consume(slot):
        kt = jnp.concatenate([kbuf[slot, i] for i in reversed(range(cp))], axis=1).astype(BF16)
        vt = jnp.concatenate([vbuf[slot, i] for i in reversed(range(cp))], axis=1).astype(BF16)
        z = _dot(qbd, kt)
        lb, lk = _sb_logs(z)
        later = _suffix_sum_exclusive(lk, upper)
        a = jnp.exp(lb + later + carry[...])
        acc[...] = acc[...] + _dot_nt(a.astype(BF16), vt)
        carry[...] = carry[...] + later[:, 0:1] + lk[:, 0:1]

    def alive():
        return jnp.max(carry[...]) > SB_DEAD_LOG

    wait(b, 0, first_slot)
    consume(first_slot)

    def cond(st):
        c, go = st
        return jnp.logical_and(c < n_chunks, go)

    def body(st):
        c, _ = st
        start(b, c, 2)
        wait(b, c, 2)
        consume(2)
        return c + 1, alive()

    lax.while_loop(cond, body, (jnp.int32(1), alive()))

    masked = jnp.where(own, acc[...], 0.0)
    for g in range(G_SB):
        o_ref[:, g * kv_w:(g + 1) * kv_w] = jnp.sum(
            masked[g * N_KV_SB:(g + 1) * N_KV_SB, :], axis=0, keepdims=True)


def _sb_sample(page_table, q, cache_kt, cache_vt):
    n, n_pages = page_table.shape
    kv_w = N_KV_SB * HD_SB
    cp = math.gcd(SB_PAGES_PER_CHUNK, n_pages)
    grid_spec = pltpu.PrefetchScalarGridSpec(
        num_scalar_prefetch=1,
        grid=(n,),
        in_specs=[pl.BlockSpec((None, 1, MIX1), lambda b, pt: (b, 0, 0)),
                  pl.BlockSpec(memory_space=pl.ANY), pl.BlockSpec(memory_space=pl.ANY)],
        out_specs=pl.BlockSpec((None, 1, MIX1), lambda b, pt: (b, 0, 0)),
        scratch_shapes=[pltpu.VMEM((3, cp, kv_w, PAGE_SIZE), F32),
                        pltpu.VMEM((3, cp, kv_w, PAGE_SIZE), F32),
                        pltpu.SemaphoreType.DMA((3, 2)),
                        pltpu.VMEM((H_SB, 1), F32), pltpu.VMEM((H_SB, kv_w), F32)],
    )
    out = pl.pallas_call(
        functools.partial(_sb_sample_kernel, pages_per_chunk=cp, n_pages=n_pages),
        grid_spec=grid_spec,
        out_shape=jax.ShapeDtypeStruct((n, 1, MIX1), F32),
        compiler_params=pltpu.CompilerParams(dimension_semantics=("arbitrary",)),
        name="sb_sample",
    )(page_table, q.reshape(n, 1, MIX1), cache_kt, cache_vt)
    return out.reshape(n, MIX1)


def _odd_out_kernel(x_ref, o_ref, sz_ref, wo_ref, g_ref, b_ref, y_ref):
    y = _dot((o_ref[...] * sz_ref[...]).astype(BF16), wo_ref[...])
    y_ref[...] = _layer_norm(DEEPNORM_ALPHA * x_ref[...] + y, g_ref[...], b_ref[...])


def _odd_out(x, o, sz, wo, g, b):
    n = x.shape[0]
    tm = min(512, n)
    row = pl.BlockSpec((tm, D_MODEL), lambda i: (i, 0))
    return pl.pallas_call(
        _odd_out_kernel,
        grid=(n // tm,),
        in_specs=[row, row, row, _full(wo.shape), _full(g.shape), _full(b.shape)],
        out_specs=row,
        out_shape=jax.ShapeDtypeStruct((n, D_MODEL), F32),
        compiler_params=pltpu.CompilerParams(
            dimension_semantics=("arbitrary",), vmem_limit_bytes=VMEM_LIMIT),
        name="odd_out",
    )(x, o, sz, wo, g, b)


def _swap_halves(w):
    half = w.shape[-1] // 2
    return jnp.concatenate([w[..., half:], w[..., :half]], axis=-1)


def _layout_w_in0(w):
    h, gc, gb, za, qa, kva, kpe, zb = jnp.split(
        w, [512, 1024, 1536, 2048, 2560, 2816, 2848], axis=1)
    pad = jnp.zeros((w.shape[0], HEAD_PAD - ROPE_DIM), w.dtype)
    cols = [h, gc, gb, za, zb, qa, kva, kpe, pad, _swap_halves(kpe), pad]
    return jnp.concatenate(cols, axis=1).astype(BF16)


def _layout_w_qb(w):
    w = w.reshape(Q_RANK, H_MLA, NOPE_DIM + ROPE_DIM)
    nope, pe = w[..., :NOPE_DIM], w[..., NOPE_DIM:]
    z32 = jnp.zeros((Q_RANK, H_MLA, HEAD_PAD - NOPE_DIM - ROPE_DIM), w.dtype)
    z96 = jnp.zeros((Q_RANK, H_MLA, HEAD_PAD - ROPE_DIM), w.dtype)
    wq = jnp.concatenate([pe, nope, z32], axis=-1).reshape(Q_RANK, H_MLA * HEAD_PAD)
    wqs = jnp.concatenate([_swap_halves(pe), z96], axis=-1).reshape(Q_RANK, H_MLA * HEAD_PAD)
    return wq.astype(BF16), wqs.astype(BF16)


def _layout_w_uk(w_uk):
    zl = jnp.zeros((KV_RANK, H_MLA, ROPE_DIM), w_uk.dtype)
    zr = jnp.zeros((KV_RANK, H_MLA, HEAD_PAD - NOPE_DIM - ROPE_DIM), w_uk.dtype)
    return jnp.concatenate([zl, w_uk, zr], axis=-1).reshape(KV_RANK, H_MLA * HEAD_PAD)


def _rope_tables(pos):
    half = ROPE_DIM // 2
    freq = ROPE_THETA ** (-jnp.arange(half, dtype=F32) / half)
    ang = pos.astype(F32)[:, None] * freq[None, :]
    cos, sin = jnp.cos(ang), jnp.sin(ang)
    n = pos.shape[0]
    ones = jnp.ones((n, NOPE_DIM), F32)
    zc = jnp.zeros((n, HEAD_PAD - NOPE_DIM - ROPE_DIM), F32)
    zs = jnp.zeros((n, HEAD_PAD - ROPE_DIM), F32)
    tc = jnp.concatenate([cos, cos, ones, zc], axis=1)
    ts = jnp.concatenate([-sin, sin, zs], axis=1)
    return tc, ts


def _layout_w_in1_prompt(w):
    kv_w = N_KV_SB * HD_SB
    q, k, v, z = jnp.split(w, [MIX1, MIX1 + kv_w, MIX1 + 2 * kv_w], axis=1)
    k3 = k.reshape(D_MODEL, N_KV_SB, HD_SB)
    v3 = v.reshape(D_MODEL, N_KV_SB, HD_SB)
    zeros = jnp.zeros_like(k3)
    ka = jnp.concatenate([k3, zeros], axis=-1).reshape(D_MODEL, N_KV_SB * LANES)
    kb = jnp.concatenate([zeros, k3], axis=-1).reshape(D_MODEL, N_KV_SB * LANES)
    vd = jnp.concatenate([v3, v3], axis=-1).reshape(D_MODEL, N_KV_SB * LANES)
    return jnp.concatenate([q, k, v, z, ka, kb, vd], axis=1).astype(BF16)


def _group_major_cols(w):
    lead = w.shape[0]
    return w.reshape(lead, N_KV_SB, G_SB, HD_SB).transpose(0, 2, 1, 3).reshape(lead, MIX1)


def _layout_w_in1_sample(w):
    kv_w = N_KV_SB * HD_SB
    q, k, v, z = jnp.split(w, [MIX1, MIX1 + kv_w, MIX1 + 2 * kv_w], axis=1)
    return jnp.concatenate([_group_major_cols(q), k, v, _group_major_cols(z)], axis=1).astype(BF16)


def kernel(x_prompt, x_sample, state_conv, cache_ckv, cache_kpe, cache_k, cache_v, page_table,
           w_in0, conv_w, q_norm, w_qb, kv_norm, w_uk, w_uv, w_out0, ln0_g, ln0_b,
           w_in1, w_out1, ln1_g, ln1_b):
    bsz, s_len, _ = x_prompt.shape
    dbsz, t_len, _ = x_sample.shape
    assert t_len == 1, "the sample group is one new token per sequence"
    n_pages = page_table.shape[1]
    p_len = n_pages * PAGE_SIZE
    n_phys = cache_ckv.shape[1]
    kv_w = N_KV_SB * HD_SB
    rows_p = bsz * s_len

    wa = _layout_w_in0(w_in0[0])
    wq, wqs = _layout_w_qb(w_qb[0])
    wuk_pad = _layout_w_uk(w_uk[0])
    wuv = w_uv[0].reshape(KV_RANK, H_MLA * V_DIM)
    wkv = jnp.concatenate([wuk_pad, wuv], axis=1).astype(BF16)
    wukt = wuk_pad.T.astype(BF16)
    wuv_b = wuv.astype(BF16)
    qn = q_norm[0].reshape(1, Q_RANK)
    kvn = kv_norm[0].reshape(1, KV_RANK)
    cw = conv_w[0]
    wo0 = w_out0[0].astype(BF16)
    g0, b0 = ln0_g[0].reshape(1, D_MODEL), ln0_b[0].reshape(1, D_MODEL)
    g1, b1 = ln1_g[0].reshape(1, D_MODEL), ln1_b[0].reshape(1, D_MODEL)
    w1p = _layout_w_in1_prompt(w_in1[0])
    w1s = _layout_w_in1_sample(w_in1[0])
    wo1 = w_out1[0].astype(BF16)
    wo1_gm = _group_major_cols(w_out1[0].T).T.astype(BF16)
    tc_p, ts_p = _rope_tables(jnp.arange(s_len))
    tc_s, ts_s = _rope_tables(jnp.full((dbsz,), p_len))

    ya, szb, q, k, v, ckv_p, kpe_p, tail = _even_in_prompt(
        x_prompt, wa, wq, wqs, wkv, qn, kvn, cw, tc_p, ts_p,
        tc_p * (MLA_SCALE * LOG2E), ts_p * (MLA_SCALE * LOG2E))
    attn = _mla_prompt(q, k, v)
    x1, q1, k1, v1, sz1, ka, kb, vd = _mid(
        x_prompt.reshape(rows_p, D_MODEL), ya.reshape(rows_p, D_CONV),
        attn.reshape(rows_p, H_MLA * V_DIM), szb.reshape(rows_p, H_MLA * V_DIM),
        wo0, g0, b0, w1p, 3, BF16)
    o1 = _sb_prompt(q1.reshape(bsz, s_len, MIX1), ka.reshape(bsz, s_len, N_KV_SB * LANES),
                    kb.reshape(bsz, s_len, N_KV_SB * LANES), vd.reshape(bsz, s_len, N_KV_SB * LANES))
    y_prompt = _odd_out(x1, o1.reshape(rows_p, MIX1), sz1, wo1, g1, b1).reshape(bsz, s_len, D_MODEL)

    xs = x_sample.reshape(dbsz, D_MODEL)
    ya_s, szb_s, q_s, ckv_s, kpe_s, u_s = _even_in_sample(
        xs, wa, wq, wqs, qn, kvn, cw, tc_s, ts_s, tc_s * MLA_SCALE, ts_s * MLA_SCALE,
        state_conv[0, :, 0, :], state_conv[0, :, 1, :])
    cache_kpet = jnp.transpose(cache_kpe[0], (0, 2, 1))
    cache_kt = jnp.transpose(cache_k[0], (0, 2, 3, 1)).reshape(n_phys, kv_w, PAGE_SIZE)
    cache_vt = jnp.transpose(cache_v[0], (0, 2, 3, 1)).reshape(n_phys, kv_w, PAGE_SIZE)
    attn_s = _mla_sample(page_table, q_s, wukt, wuv_b, ckv_s, kpe_s, cache_ckv[0], cache_kpet)
    x1_s, q1_s, k1_s, v1_s, sz1_s = _mid(xs, ya_s, attn_s, szb_s, wo0, g0, b0, w1s, 0, F32)
    o1_s = _sb_sample(page_table, q1_s, cache_kt, cache_vt)
    y_sample = _odd_out(x1_s, o1_s, sz1_s, wo1_gm, g1, b1).reshape(dbsz, 1, D_MODEL)

    conv_prompt = tail[:, SUBLANES - (CONV_W - 1):, :][None]
    conv_sample = jnp.stack([state_conv[0, :, 1, :], u_s], axis=1)[None]
    return (y_prompt, y_sample, conv_prompt, conv_sample,
            ckv_p[None], ckv_s.reshape(1, dbsz, 1, KV_RANK),
            kpe_p[None], kpe_s.reshape(1, dbsz, 1, ROPE_DIM),
            k1.reshape(1, bsz, s_len, N_KV_SB, HD_SB), k1_s.reshape(1, dbsz, 1, N_KV_SB, HD_SB),
            v1.reshape(1, bsz, s_len, N_KV_SB, HD_SB), v1_s.reshape(1, dbsz, 1, N_KV_SB, HD_SB))
```

```python
import functools
import math

import jax
import jax.numpy as jnp
from jax import lax
from jax.experimental import pallas as pl
from jax.experimental.pallas import tpu as pltpu

F32 = jnp.float32
BF16 = jnp.bfloat16

D_MODEL = 1024
D_CONV = 512
CONV_W = 3
H_MLA = 8
Q_RANK = 512
KV_RANK = 256
NOPE_DIM = 64
ROPE_DIM = 32
V_DIM = 64
ROPE_THETA = 10000.0
MLA_SCALE = (NOPE_DIM + ROPE_DIM) ** -0.5
H_SB = 16
N_KV_SB = 4
HD_SB = 64
G_SB = H_SB // N_KV_SB
SB_SCALE = HD_SB ** -0.5
MIX1 = H_SB * HD_SB
PAGE_SIZE = 128
DEPTH = 2
DEEPNORM_ALPHA = (2 * DEPTH) ** 0.25
EPS = 1e-5

LANES = 128
SUBLANES = 8
HEAD_PAD = LANES
VMEM_LIMIT = 56 * 1024 * 1024

SB_DEAD_LOG = -104.0

LOG2E = math.log2(math.e)

MLA_TQ = 512
MLA_TK = 1024

MLA_PAGES_PER_CHUNK = 64
SB_PAGES_PER_CHUNK = 2

A_H, A_GC, A_GB, A_ZA, A_ZB, A_QA, A_KVA, A_KPE, A_KPES, A_END = (
    0, 512, 1024, 1536, 2048, 2560, 3072, 3328, 3456, 3584)


def _silu(x):
    return x * jax.nn.sigmoid(x)


def _dot(a, b):
    return jnp.dot(a, b, preferred_element_type=F32)


def _dot_nt(a, b):
    return lax.dot_general(a, b, (((1,), (1,)), ((), ())), preferred_element_type=F32)


def _rms(x, g):
    return x * lax.rsqrt(jnp.mean(jnp.square(x), axis=-1, keepdims=True) + EPS) * g


def _layer_norm(x, g, b):
    mu = jnp.mean(x, axis=-1, keepdims=True)
    xc = x - mu
    var = jnp.mean(jnp.square(xc), axis=-1, keepdims=True)
    return xc * lax.rsqrt(var + EPS) * g + b


def _even_in_common(xb, wa_ref, wq_ref, wqs_ref, qn_ref, kvn_ref, tc, ts, tcq, tsq,
                    q_ref, ckv_ref, kpe_ref, q_feature_major):
    def proj(a, b):
        return _dot(xb, wa_ref[:, a:b])

    u = proj(A_GC, A_GB) * proj(A_H, A_GC)
    qn = _rms(proj(A_QA, A_KVA), qn_ref[...]).astype(BF16)
    qf = _dot(qn, wq_ref[...])
    qs = _dot(qn, wqs_ref[...])
    for h in range(H_MLA):
        sl = slice(h * HEAD_PAD, (h + 1) * HEAD_PAD)
        qh = qf[:, sl] * tcq + qs[:, sl] * tsq
        if q_feature_major:
            q_ref[sl, :] = qh.T.astype(q_ref.dtype)
        else:
            q_ref[:, sl] = qh.astype(q_ref.dtype)
    ckv = _rms(proj(A_KVA, A_KPE), kvn_ref[...])
    ckv_ref[...] = ckv
    kpe128 = proj(A_KPE, A_KPES) * tc + proj(A_KPES, A_END) * ts
    kpe_ref[...] = kpe128[:, :ROPE_DIM]
    return u, proj(A_GB, A_ZA), proj(A_ZA, A_ZB), proj(A_ZB, A_QA), ckv, kpe128


def _even_in_prompt_kernel(x_ref, wa_ref, wq_ref, wqs_ref, wkv_ref, qn_ref, kvn_ref, cw_ref,
                           tc_ref, ts_ref, tcq_ref, tsq_ref,
                           ya_ref, szb_ref, qt_ref, k_ref, vt_ref, ckv_ref, kpe_ref, tail_ref,
                           ubuf, *, tm):
    i = pl.program_id(1)
    xb = x_ref[...].astype(BF16)
    u, gb, za, zb, ckv, kpe128 = _even_in_common(
        xb, wa_ref, wq_ref, wqs_ref, qn_ref, kvn_ref, tc_ref[...], ts_ref[...],
        tcq_ref[...], tsq_ref[...], qt_ref, ckv_ref, kpe_ref, True)

    @pl.when(i == 0)
    def _():
        ubuf[0:SUBLANES, :] = jnp.zeros((SUBLANES, D_CONV), F32)

    @pl.when(i > 0)
    def _():
        ubuf[0:SUBLANES, :] = ubuf[tm:tm + SUBLANES, :]

    ubuf[SUBLANES:tm + SUBLANES, :] = u
    cw = cw_ref[...]
    conv = (cw[0:1, :] * ubuf[SUBLANES - 2:tm + SUBLANES - 2, :]
            + cw[1:2, :] * ubuf[SUBLANES - 1:tm + SUBLANES - 1, :]
            + cw[2:3, :] * u)
    ya_ref[...] = (gb * conv * _silu(za)).astype(ya_ref.dtype)
    szb_ref[...] = _silu(zb)
    tail_ref[...] = ubuf[tm:tm + SUBLANES, :]

    ckvb = ckv.astype(BF16)
    kn = _dot(ckvb, wkv_ref[:, :H_MLA * HEAD_PAD])
    vt_ref[...] = _dot(ckvb, wkv_ref[:, H_MLA * HEAD_PAD:]).T.astype(vt_ref.dtype)
    for h in range(H_MLA):
        sl = slice(h * HEAD_PAD, (h + 1) * HEAD_PAD)
        k_ref[:, sl] = (kn[:, sl] + kpe128).astype(k_ref.dtype)


def _even_in_sample_kernel(x_ref, wa_ref, wq_ref, wqs_ref, qn_ref, kvn_ref, cw_ref,
                           tc_ref, ts_ref, tcq_ref, tsq_ref, s0_ref, s1_ref,
                           ya_ref, szb_ref, q_ref, ckv_ref, kpe_ref, u_ref):
    xb = x_ref[...].astype(BF16)
    u, gb, za, zb, _, _ = _even_in_common(
        xb, wa_ref, wq_ref, wqs_ref, qn_ref, kvn_ref, tc_ref[...], ts_ref[...],
        tcq_ref[...], tsq_ref[...], q_ref, ckv_ref, kpe_ref, False)
    cw = cw_ref[...]
    conv = cw[0:1, :] * s0_ref[...] + cw[1:2, :] * s1_ref[...] + cw[2:3, :] * u
    ya_ref[...] = (gb * conv * _silu(za)).astype(ya_ref.dtype)
    szb_ref[...] = _silu(zb)
    u_ref[...] = u


def _full(shape):
    nd = len(shape)
    return pl.BlockSpec(shape, lambda *_: (0,) * nd)


def _even_in_prompt(x, wa, wq, wqs, wkv, qn, kvn, cw, tc, ts, tcq, tsq):
    bsz, s, _ = x.shape
    tm = min(512, s)
    nt = s // tm
    row = lambda w: pl.BlockSpec((None, tm, w), lambda b, i: (b, i, 0))
    col = lambda w: pl.BlockSpec((None, w, tm), lambda b, i: (b, 0, i))
    tab = pl.BlockSpec((tm, LANES), lambda b, i: (i, 0))
    outs = (
        jax.ShapeDtypeStruct((bsz, s, D_CONV), BF16),
        jax.ShapeDtypeStruct((bsz, s, H_MLA * V_DIM), F32),
        jax.ShapeDtypeStruct((bsz, H_MLA * HEAD_PAD, s), BF16),
        jax.ShapeDtypeStruct((bsz, s, H_MLA * HEAD_PAD), BF16),
        jax.ShapeDtypeStruct((bsz, H_MLA * V_DIM, s), BF16),
        jax.ShapeDtypeStruct((bsz, s, KV_RANK), F32),
        jax.ShapeDtypeStruct((bsz, s, ROPE_DIM), F32),
        jax.ShapeDtypeStruct((bsz, SUBLANES, D_CONV), F32),
    )
    return pl.pallas_call(
        functools.partial(_even_in_prompt_kernel, tm=tm),
        grid=(bsz, nt),
        in_specs=[row(D_MODEL), _full(wa.shape), _full(wq.shape), _full(wqs.shape), _full(wkv.shape),
                  _full(qn.shape), _full(kvn.shape), _full(cw.shape), tab, tab, tab, tab],
        out_specs=[row(D_CONV), row(H_MLA * V_DIM), col(H_MLA * HEAD_PAD), row(H_MLA * HEAD_PAD),
                   col(H_MLA * V_DIM), row(KV_RANK), row(ROPE_DIM),
                   pl.BlockSpec((None, SUBLANES, D_CONV), lambda b, i: (b, 0, 0))],
        out_shape=outs,
        scratch_shapes=[pltpu.VMEM((tm + SUBLANES, D_CONV), F32)],
        compiler_params=pltpu.CompilerParams(
            dimension_semantics=("arbitrary", "arbitrary"), vmem_limit_bytes=VMEM_LIMIT),
        name="even_in_prompt",
    )(x, wa, wq, wqs, wkv, qn, kvn, cw, tc, ts, tcq, tsq)


def _even_in_sample(x, wa, wq, wqs, qn, kvn, cw, tc, ts, tcq, tsq, s0, s1):
    n = x.shape[0]
    outs = (
        jax.ShapeDtypeStruct((n, D_CONV), BF16),
        jax.ShapeDtypeStruct((n, H_MLA * V_DIM), F32),
        jax.ShapeDtypeStruct((n, H_MLA * HEAD_PAD), F32),
        jax.ShapeDtypeStruct((n, KV_RANK), F32),
        jax.ShapeDtypeStruct((n, ROPE_DIM), F32),
        jax.ShapeDtypeStruct((n, D_CONV), F32),
    )
    args = (x, wa, wq, wqs, qn, kvn, cw, tc, ts, tcq, tsq, s0, s1)
    return pl.pallas_call(
        _even_in_sample_kernel,
        grid=(1,),
        in_specs=[_full(a.shape) for a in args],
        out_specs=[_full(o.shape) for o in outs],
        out_shape=outs,
        compiler_params=pltpu.CompilerParams(vmem_limit_bytes=VMEM_LIMIT),
        name="even_in_sample",
    )(*args)


def _mla_prompt_kernel(qt_ref, k_ref, vt_ref, o_ref, *, tq, tk):
    qi = pl.program_id(2)
    key_minus_query = (lax.broadcasted_iota(jnp.int32, (tk, tq), 0)
                       - lax.broadcasted_iota(jnp.int32, (tk, tq), 1))

    def block(j, carry, masked):
        start = pl.multiple_of(j * tk, tk)
        new = []
        for e in range(2):
            m, l, acc = carry[e]
            s = _dot(k_ref[pl.ds(start, tk), e * HEAD_PAD:(e + 1) * HEAD_PAD],
                     qt_ref[e * HEAD_PAD:(e + 1) * HEAD_PAD, :])
            if masked:
                s = jnp.where(key_minus_query <= qi * tq - j * tk, s, -jnp.inf)
            m_new = jnp.maximum(m, jnp.max(s, axis=0, keepdims=True))
            p = jnp.exp2(s - m_new)
            alpha = jnp.exp2(m - m_new)
            l = alpha * l + jnp.sum(p, axis=0, keepdims=True)
            acc = alpha * acc + _dot(vt_ref[e * V_DIM:(e + 1) * V_DIM, pl.ds(start, tk)], p.astype(BF16))
            new.append((m_new, l, acc))
        return tuple(new)

    init = tuple((jnp.full((1, tq), -jnp.inf, F32), jnp.zeros((1, tq), F32),
                  jnp.zeros((V_DIM, tq), F32)) for _ in range(2))
    n_full = (qi * tq + 1) // tk
    n_end = ((qi + 1) * tq + tk - 1) // tk
    carry = lax.fori_loop(0, n_full, lambda j, c: block(j, c, False), init)
    carry = lax.fori_loop(n_full, n_end, lambda j, c: block(j, c, True), carry)
    ot = jnp.concatenate([carry[e][2] / carry[e][1] for e in range(2)], axis=0)
    o_ref[...] = ot.T


def _mla_prompt(qt, k, vt):
    bsz, s, _ = k.shape
    tq, tk = min(MLA_TQ, s), min(MLA_TK, s)
    nq = s // tq
    return pl.pallas_call(
        functools.partial(_mla_prompt_kernel, tq=tq, tk=tk),
        grid=(bsz, H_MLA // 2, nq),
        in_specs=[pl.BlockSpec((None, 2 * HEAD_PAD, tq), lambda b, h, i: (b, h, i)),
                  pl.BlockSpec((None, s, 2 * HEAD_PAD), lambda b, h, i: (b, 0, h)),
                  pl.BlockSpec((None, 2 * V_DIM, s), lambda b, h, i: (b, h, 0))],
        out_specs=pl.BlockSpec((None, tq, 2 * V_DIM), lambda b, h, i: (b, i, h)),
        out_shape=jax.ShapeDtypeStruct((bsz, s, H_MLA * V_DIM), F32),
        compiler_params=pltpu.CompilerParams(
            dimension_semantics=("arbitrary", "arbitrary", "arbitrary"), vmem_limit_bytes=VMEM_LIMIT),
        name="mla_prompt",
    )(qt, k, vt)


def _mla_sample_kernel(pt_ref, q_ref, wukt_ref, wuv_ref, ckvn_ref, kpen_ref, ckv_hbm, kpet_hbm,
                       o_ref, ckv_buf, kpe_buf, sems, *, pages_per_chunk, n_chunks):
    b = pl.program_id(0)
    nb = pl.num_programs(0)
    cp = pages_per_chunk

    def chunk_copies(seq, c, slot):
        copies = []
        for i in range(cp):
            page = pt_ref[seq, c * cp + i]
            copies.append(pltpu.make_async_copy(ckv_hbm.at[page], ckv_buf.at[slot, i], sems.at[slot, 0]))
            copies.append(pltpu.make_async_copy(kpet_hbm.at[page], kpe_buf.at[slot, i], sems.at[slot, 1]))
        return copies

    def start(seq, c, slot):
        for copy in chunk_copies(seq, c, slot):
            copy.start()

    def wait(seq, c, slot):
        for copy in chunk_copies(seq, c, slot):
            copy.wait()

    @pl.when(b == 0)
    def _():
        start(0, 0, 0)

    head_of_group = lax.broadcasted_iota(jnp.int32, (H_MLA, H_MLA * HEAD_PAD), 1) // HEAD_PAD
    row8 = lax.broadcasted_iota(jnp.int32, (H_MLA, H_MLA * HEAD_PAD), 0)
    qrow = q_ref[...]
    qbd = jnp.where(head_of_group == row8, jnp.broadcast_to(qrow, row8.shape), 0.0)
    qlat = _dot(qbd.astype(BF16), wukt_ref[...])
    q8 = jnp.zeros((H_MLA, HEAD_PAD), F32)
    sub = lax.broadcasted_iota(jnp.int32, (H_MLA, HEAD_PAD), 0)
    for h in range(H_MLA):
        q8 = jnp.where(sub == h, jnp.broadcast_to(qrow[:, h * HEAD_PAD:(h + 1) * HEAD_PAD],
                                                  (H_MLA, HEAD_PAD)), q8)
    qpe = q8[:, :ROPE_DIM]
    qlat_b = qlat.astype(BF16)
    qpe_b = qpe.astype(BF16)

    def chunk(c, carry):
        m_old, l_old, acc = carry
        slot = (b * n_chunks + c) % 2

        @pl.when(c + 1 < n_chunks)
        def _():
            start(b, c + 1, 1 - slot)

        @pl.when(jnp.logical_and(c + 1 == n_chunks, b + 1 < nb))
        def _():
            start(b + 1, 0, 1 - slot)

        wait(b, c, slot)
        ckv = ckv_buf[slot].reshape(cp * PAGE_SIZE, KV_RANK).astype(BF16)
        kpet = jnp.concatenate([kpe_buf[slot, i] for i in range(cp)], axis=1).astype(BF16)
        s = _dot_nt(qlat_b, ckv) + _dot(qpe_b, kpet)
        m_new = jnp.maximum(m_old, jnp.max(s, axis=-1, keepdims=True))
        p = jnp.exp(s - m_new)
        alpha = jnp.exp(m_old - m_new)
        l_new = alpha * l_old + jnp.sum(p, axis=-1, keepdims=True)
        acc = alpha * acc + _dot(p.astype(BF16), ckv)
        return m_new, l_new, acc

    init = (jnp.full((H_MLA, 1), -jnp.inf, F32), jnp.zeros((H_MLA, 1), F32),
            jnp.zeros((H_MLA, KV_RANK), F32))
    m_old, l_old, acc = lax.fori_loop(0, n_chunks, chunk, init)

    ckv_new = ckvn_ref[...]
    kpe_new = kpen_ref[...]
    s_new = (jnp.sum(qlat * ckv_new, axis=-1, keepdims=True)
             + jnp.sum(qpe * kpe_new, axis=-1, keepdims=True))
    m_new = jnp.maximum(m_old, s_new)
    p_new = jnp.exp(s_new - m_new)
    alpha = jnp.exp(m_old - m_new)
    l = alpha * l_old + p_new
    o_lat = (alpha * acc + p_new * ckv_new) / l
    full = _dot(o_lat.astype(BF16), wuv_ref[...])
    hv = lax.broadcasted_iota(jnp.int32, full.shape, 1) // V_DIM
    rv = lax.broadcasted_iota(jnp.int32, full.shape, 0)
    o_ref[...] = jnp.sum(jnp.where(hv == rv, full, 0.0), axis=0, keepdims=True)


def _mla_sample(page_table, q, wukt, wuv, ckv_new, kpe_new, cache_ckv, cache_kpet):
    n, n_pages = page_table.shape
    q3 = q.reshape(n, 1, H_MLA * HEAD_PAD)
    ckv3 = ckv_new.reshape(n, 1, KV_RANK)
    kpe3 = kpe_new.reshape(n, 1, ROPE_DIM)
    cp = math.gcd(MLA_PAGES_PER_CHUNK, n_pages)
    grid_spec = pltpu.PrefetchScalarGridSpec(
        num_scalar_prefetch=1,
        grid=(n,),
        in_specs=[
            pl.BlockSpec((None, 1, H_MLA * HEAD_PAD), lambda b, pt: (b, 0, 0)),
            pl.BlockSpec(wukt.shape, lambda b, pt: (0, 0)),
            pl.BlockSpec(wuv.shape, lambda b, pt: (0, 0)),
            pl.BlockSpec((None, 1, KV_RANK), lambda b, pt: (b, 0, 0)),
            pl.BlockSpec((None, 1, ROPE_DIM), lambda b, pt: (b, 0, 0)),
            pl.BlockSpec(memory_space=pl.ANY),
            pl.BlockSpec(memory_space=pl.ANY),
        ],
        out_specs=pl.BlockSpec((None, 1, H_MLA * V_DIM), lambda b, pt: (b, 0, 0)),
        scratch_shapes=[pltpu.VMEM((2, cp, PAGE_SIZE, KV_RANK), F32),
                        pltpu.VMEM((2, cp, ROPE_DIM, PAGE_SIZE), F32),
                        pltpu.SemaphoreType.DMA((2, 2))],
    )
    out = pl.pallas_call(
        functools.partial(_mla_sample_kernel, pages_per_chunk=cp, n_chunks=n_pages // cp),
        grid_spec=grid_spec,
        out_shape=jax.ShapeDtypeStruct((n, 1, H_MLA * V_DIM), F32),
        compiler_params=pltpu.CompilerParams(
            dimension_semantics=("arbitrary",), vmem_limit_bytes=VMEM_LIMIT),
        name="mla_sample",
    )(page_table, q3, wukt, wuv, ckv3, kpe3, cache_ckv, cache_kpet)
    return out.reshape(n, H_MLA * V_DIM)


def _mid_kernel(x_ref, ya_ref, attn_ref, szb_ref, wo_ref, g_ref, b_ref, w1_ref,
                x1_ref, q_ref, k_ref, v_ref, sz_ref, *sb_refs):
    yb = (attn_ref[...] * szb_ref[...]).astype(BF16)
    y = _dot(ya_ref[...], wo_ref[:D_CONV, :]) + _dot(yb, wo_ref[D_CONV:, :])
    x1 = _layer_norm(DEEPNORM_ALPHA * x_ref[...] + y, g_ref[...], b_ref[...])
    x1_ref[...] = x1
    xb = x1.astype(BF16)
    kv_w = N_KV_SB * HD_SB
    q_ref[...] = (_dot(xb, w1_ref[:, :MIX1]) * SB_SCALE).astype(q_ref.dtype)
    k = _dot(xb, w1_ref[:, MIX1:MIX1 + kv_w])
    v = _dot(xb, w1_ref[:, MIX1 + kv_w:MIX1 + 2 * kv_w])
    sz_ref[...] = _silu(_dot(xb, w1_ref[:, MIX1 + 2 * kv_w:]))
    if not sb_refs:
        k_ref[...] = k
        v_ref[...] = v
        return
    k_ref[...] = k.T
    v_ref[...] = v.T
    ka_ref, kb_ref, vd_ref = sb_refs
    low = lax.broadcasted_iota(jnp.int32, (k.shape[0], LANES), 1) < HD_SB
    for s in range(N_KV_SB // 2):
        ks = k[:, s * LANES:(s + 1) * LANES]
        vs = v[:, s * LANES:(s + 1) * LANES]
        kr = pltpu.roll(ks, HD_SB, 1)
        vr = pltpu.roll(vs, HD_SB, 1)
        even = slice(2 * s * LANES, (2 * s + 1) * LANES)
        odd = slice((2 * s + 1) * LANES, (2 * s + 2) * LANES)
        ka_ref[:, even] = jnp.where(low, ks, 0.0).astype(BF16)
        ka_ref[:, odd] = jnp.where(low, kr, 0.0).astype(BF16)
        kb_ref[:, even] = jnp.where(low, 0.0, kr).astype(BF16)
        kb_ref[:, odd] = jnp.where(low, 0.0, ks).astype(BF16)
        vd_ref[:, even] = jnp.where(low, vs, vr).astype(BF16)
        vd_ref[:, odd] = jnp.where(low, vr, vs).astype(BF16)


def _mid(x, ya, attn, szb, wo, g, b, w1, seq, q_dtype):
    n = x.shape[0]
    tm = min(512, n)
    kv_w = N_KV_SB * HD_SB
    row = lambda w: pl.BlockSpec((tm, w), lambda i: (i, 0))
    outs = [jax.ShapeDtypeStruct((n, D_MODEL), F32), jax.ShapeDtypeStruct((n, MIX1), q_dtype)]
    specs = [row(D_MODEL), row(MIX1)]
    if seq is None:
        outs += [jax.ShapeDtypeStruct((n, kv_w), F32)] * 2 + [jax.ShapeDtypeStruct((n, MIX1), F32)]
        specs += [row(kv_w), row(kv_w), row(MIX1)]
    else:
        nt = seq // tm
        feat = pl.BlockSpec((None, kv_w, tm), lambda i: (i // nt, 0, i % nt))
        outs += [jax.ShapeDtypeStruct((n // seq, kv_w, seq), F32)] * 2 + [jax.ShapeDtypeStruct((n, MIX1), F32)]
        outs += [jax.ShapeDtypeStruct((n, N_KV_SB * LANES), BF16)] * 3
        specs += [feat, feat, row(MIX1)] + [row(N_KV_SB * LANES)] * 3
    return pl.pallas_call(
        _mid_kernel,
        grid=(n // tm,),
        in_specs=[row(D_MODEL), row(D_CONV), row(H_MLA * V_DIM), row(H_MLA * V_DIM),
                  _full(wo.shape), _full(g.shape), _full(b.shape), _full(w1.shape)],
        out_specs=specs,
        out_shape=outs,
        compiler_params=pltpu.CompilerParams(
            dimension_semantics=("arbitrary",), vmem_limit_bytes=VMEM_LIMIT),
        name="even_out_odd_in",
    )(x, ya, attn, szb, wo, g, b, w1)


def _sb_logs(z):
    sp = jnp.log(1.0 + jnp.exp(-jnp.abs(z)))
    lb = jnp.minimum(z, 0.0) - sp
    return lb, lb - z


def _upper2(n):
    r = lax.broadcasted_iota(jnp.int32, (2 * n, n), 0)
    c = lax.broadcasted_iota(jnp.int32, (2 * n, n), 1)
    return jnp.where(jnp.where(r >= n, r - n, r) > c, 1.0, 0.0).astype(BF16)


def _suffix_sum_exclusive(x, upper2):
    hi = x.astype(BF16)
    lo = (x - hi.astype(F32)).astype(BF16)
    return _dot(jnp.concatenate([hi, lo], axis=1), upper2)


def _sb_prompt_kernel(q_ref, ka_ref, kb_ref, v_ref, o_ref, *, tq):
    qi = pl.program_id(2)
    upper = _upper2(tq)
    q2 = jnp.concatenate([q_ref[:, :LANES], q_ref[:, LANES:]], axis=0)
    rows = G_SB * tq
    row_in_tile = lax.rem(lax.broadcasted_iota(jnp.int32, (rows, tq), 0), tq)
    diag_mask = lax.broadcasted_iota(jnp.int32, (rows, tq), 1) < row_in_tile

    def block(j, state, masked):
        carry, acc = state
        start = pl.multiple_of(j * tq, tq)
        z = jnp.concatenate([_dot_nt(q2, ka_ref[pl.ds(start, tq), :]),
                             _dot_nt(q2, kb_ref[pl.ds(start, tq), :])], axis=0)
        lb, lk = _sb_logs(z)
        if masked:
            lk = jnp.where(diag_mask, lk, 0.0)
        later = _suffix_sum_exclusive(lk, upper)
        a = jnp.exp(lb + later + carry)
        if masked:
            a = jnp.where(diag_mask, a, 0.0)
        acc = acc + _dot(a.astype(BF16), v_ref[pl.ds(start, tq), :])
        carry = carry + later[:, 0:1] + lk[:, 0:1]
        return carry, acc

    def alive(st):
        return jnp.max(st[0]) > SB_DEAD_LOG

    def cond(ls):
        j, st, go = ls
        return jnp.logical_and(j >= 0, go)

    def body(ls):
        j, st, _ = ls
        st = block(j, st, False)
        return j - 1, st, alive(st)

    state = block(qi, (jnp.zeros((rows, 1), F32), jnp.zeros((rows, LANES), F32)), True)
    _, (_, acc), _ = lax.while_loop(cond, body, (qi - 1, state, alive(state)))
    lane = lax.broadcasted_iota(jnp.int32, (tq, LANES), 1)
    o_ref[:, :LANES] = jnp.where(lane < HD_SB, acc[0:tq], acc[2 * tq:3 * tq])
    o_ref[:, LANES:] = jnp.where(lane < HD_SB, acc[tq:2 * tq], acc[3 * tq:4 * tq])


def _sb_prompt(q, ka, kb, vdup):
    bsz, s, _ = q.shape
    tq = min(256, s)
    nq = s // tq
    kvspec = pl.BlockSpec((None, s, LANES), lambda b, j, i: (b, 0, j))
    return pl.pallas_call(
        functools.partial(_sb_prompt_kernel, tq=tq),
        grid=(bsz, N_KV_SB, nq),
        in_specs=[pl.BlockSpec((None, tq, G_SB * HD_SB), lambda b, j, i: (b, i, j)),
                  kvspec, kvspec, kvspec],
        out_specs=pl.BlockSpec((None, tq, G_SB * HD_SB), lambda b, j, i: (b, i, j)),
        out_shape=jax.ShapeDtypeStruct((bsz, s, MIX1), F32),
        compiler_params=pltpu.CompilerParams(
            dimension_semantics=("arbitrary", "arbitrary", "arbitrary"), vmem_limit_bytes=VMEM_LIMIT),
        name="sb_prompt",
    )(q, ka, kb, vdup)


def _sb_sample_kernel(pt_ref, q_ref, kt_hbm, vt_hbm, o_ref, kbuf, vbuf, sems, carry, acc,
                      *, pages_per_chunk, n_pages):
    b = pl.program_id(0)
    nb = pl.num_programs(0)
    cp = pages_per_chunk
    n_chunks = n_pages // cp
    kv_w = N_KV_SB * HD_SB
    keys = cp * PAGE_SIZE

    def chunk_copies(seq, c, slot):
        copies = []
        for i in range(cp):
            page = pt_ref[seq, n_pages - 1 - (c * cp + i)]
            copies.append(pltpu.make_async_copy(kt_hbm.at[page], kbuf.at[slot, i], sems.at[slot, 0]))
            copies.append(pltpu.make_async_copy(vt_hbm.at[page], vbuf.at[slot, i], sems.at[slot, 1]))
        return copies

    def start(seq, c, slot):
        for copy in chunk_copies(seq, c, slot):
            copy.start()

    def wait(seq, c, slot):
        for copy in chunk_copies(seq, c, slot):
            copy.wait()

    first_slot = b % 2

    @pl.when(b == 0)
    def _():
        start(0, 0, 0)

    @pl.when(b + 1 < nb)
    def _():
        start(b + 1, 0, 1 - first_slot)

    kv_of_lane = lax.broadcasted_iota(jnp.int32, (H_SB, kv_w), 1) // HD_SB
    kv_of_row = lax.broadcasted_iota(jnp.int32, (H_SB, kv_w), 0) % N_KV_SB
    own = kv_of_lane == kv_of_row
    qrow = q_ref[...]
    parts = [jnp.broadcast_to(qrow[:, g * kv_w:(g + 1) * kv_w], (N_KV_SB, kv_w)) for g in range(G_SB)]
    qbd = jnp.where(own, jnp.concatenate(parts, axis=0), 0.0).astype(BF16)
    upper = _upper2(keys)
    carry[...] = jnp.zeros(carry.shape, F32)
    acc[...] = jnp.zeros(acc.shape, F32)

    def consume(slot):
        kt = jnp.concatenate([kbuf[slot, i] for i in reversed(range(cp))], axis=1).astype(BF16)
        vt = jnp.concatenate([vbuf[slot, i] for i in reversed(range(cp))], axis=1).astype(BF16)
        z = _dot(qbd, kt)
        lb, lk = _sb_logs(z)
        later = _suffix_sum_exclusive(lk, upper)
        a = jnp.exp(lb + later + carry[...])
        acc[...] = acc[...] + _dot_nt(a.astype(BF16), vt)
        carry[...] = carry[...] + later[:, 0:1] + lk[:, 0:1]

    def alive():
        return jnp.max(carry[...]) > SB_DEAD_LOG

    wait(b, 0, first_slot)
    consume(first_slot)

    def cond(st):
        c, go = st
        return jnp.logical_and(c < n_chunks, go)

    def body(st):
        c, _ = st
        start(b, c, 2)
        wait(b, c, 2)
        consume(2)
        return c + 1, alive()

    lax.while_loop(cond, body, (jnp.int32(1), alive()))

    masked = jnp.where(own, acc[...], 0.0)
    for g in range(G_SB):
        o_ref[:, g * kv_w:(g + 1) * kv_w] = jnp.sum(
            masked[g * N_KV_SB:(g + 1) * N_KV_SB, :], axis=0, keepdims=True)


def _sb_sample(page_table, q, cache_kt, cache_vt):
    n, n_pages = page_table.shape
    kv_w = N_KV_SB * HD_SB
    cp = math.gcd(SB_PAGES_PER_CHUNK, n_pages)
    grid_spec = pltpu.PrefetchScalarGridSpec(
        num_scalar_prefetch=1,
        grid=(n,),
        in_specs=[pl.BlockSpec((None, 1, MIX1), lambda b, pt: (b, 0, 0)),
                  pl.BlockSpec(memory_space=pl.ANY), pl.BlockSpec(memory_space=pl.ANY)],
        out_specs=pl.BlockSpec((None, 1, MIX1), lambda b, pt: (b, 0, 0)),
        scratch_shapes=[pltpu.VMEM((3, cp, kv_w, PAGE_SIZE), F32),
                        pltpu.VMEM((3, cp, kv_w, PAGE_SIZE), F32),
                        pltpu.SemaphoreType.DMA((3, 2)),
                        pltpu.VMEM((H_SB, 1), F32), pltpu.VMEM((H_SB, kv_w), F32)],
    )
    out = pl.pallas_call(
        functools.partial(_sb_sample_kernel, pages_per_chunk=cp, n_pages=n_pages),
        grid_spec=grid_spec,
        out_shape=jax.ShapeDtypeStruct((n, 1, MIX1), F32),
        compiler_params=pltpu.CompilerParams(dimension_semantics=("arbitrary",)),
        name="sb_sample",
    )(page_table, q.reshape(n, 1, MIX1), cache_kt, cache_vt)
    return out.reshape(n, MIX1)


def _odd_out_kernel(x_ref, o_ref, sz_ref, wo_ref, g_ref, b_ref, y_ref):
    y = _dot((o_ref[...] * sz_ref[...]).astype(BF16), wo_ref[...])
    y_ref[...] = _layer_norm(DEEPNORM_ALPHA * x_ref[...] + y, g_ref[...], b_ref[...])


def _odd_out(x, o, sz, wo, g, b):
    n = x.shape[0]
    tm = min(512, n)
    row = pl.BlockSpec((tm, D_MODEL), lambda i: (i, 0))
    return pl.pallas_call(
        _odd_out_kernel,
        grid=(n // tm,),
        in_specs=[row, row, row, _full(wo.shape), _full(g.shape), _full(b.shape)],
        out_specs=row,
        out_shape=jax.ShapeDtypeStruct((n, D_MODEL), F32),
        compiler_params=pltpu.CompilerParams(
            dimension_semantics=("arbitrary",), vmem_limit_bytes=VMEM_LIMIT),
        name="odd_out",
    )(x, o, sz, wo, g, b)


def _swap_halves(w):
    half = w.shape[-1] // 2
    return jnp.concatenate([w[..., half:], w[..., :half]], axis=-1)


def _layout_w_in0(w):
    h, gc, gb, za, qa, kva, kpe, zb = jnp.split(
        w, [512, 1024, 1536, 2048, 2560, 2816, 2848], axis=1)
    pad = jnp.zeros((w.shape[0], HEAD_PAD - ROPE_DIM), w.dtype)
    cols = [h, gc, gb, za, zb, qa, kva, kpe, pad, _swap_halves(kpe), pad]
    return jnp.concatenate(cols, axis=1).astype(BF16)


def _layout_w_qb(w):
    w = w.reshape(Q_RANK, H_MLA, NOPE_DIM + ROPE_DIM)
    nope, pe = w[..., :NOPE_DIM], w[..., NOPE_DIM:]
    z32 = jnp.zeros((Q_RANK, H_MLA, HEAD_PAD - NOPE_DIM - ROPE_DIM), w.dtype)
    z96 = jnp.zeros((Q_RANK, H_MLA, HEAD_PAD - ROPE_DIM), w.dtype)
    wq = jnp.concatenate([pe, nope, z32], axis=-1).reshape(Q_RANK, H_MLA * HEAD_PAD)
    wqs = jnp.concatenate([_swap_halves(pe), z96], axis=-1).reshape(Q_RANK, H_MLA * HEAD_PAD)
    return wq.astype(BF16), wqs.astype(BF16)


def _layout_w_uk(w_uk):
    zl = jnp.zeros((KV_RANK, H_MLA, ROPE_DIM), w_uk.dtype)
    zr = jnp.zeros((KV_RANK, H_MLA, HEAD_PAD - NOPE_DIM - ROPE_DIM), w_uk.dtype)
    return jnp.concatenate([zl, w_uk, zr], axis=-1).reshape(KV_RANK, H_MLA * HEAD_PAD)


def _rope_tables(pos):
    half = ROPE_DIM // 2
    freq = ROPE_THETA ** (-jnp.arange(half, dtype=F32) / half)
    ang = pos.astype(F32)[:, None] * freq[None, :]
    cos, sin = jnp.cos(ang), jnp.sin(ang)
    n = pos.shape[0]
    ones = jnp.ones((n, NOPE_DIM), F32)
    zc = jnp.zeros((n, HEAD_PAD - NOPE_DIM - ROPE_DIM), F32)
    zs = jnp.zeros((n, HEAD_PAD - ROPE_DIM), F32)
    tc = jnp.concatenate([cos, cos, ones, zc], axis=1)
    ts = jnp.concatenate([-sin, sin, zs], axis=1)
    return tc, ts


def _group_major_cols(w):
    lead = w.shape[0]
    return w.reshape(lead, N_KV_SB, G_SB, HD_SB).transpose(0, 2, 1, 3).reshape(lead, MIX1)


def _layout_w_in1_sample(w):
    kv_w = N_KV_SB * HD_SB
    q, k, v, z = jnp.split(w, [MIX1, MIX1 + kv_w, MIX1 + 2 * kv_w], axis=1)
    return jnp.concatenate([_group_major_cols(q), k, v, _group_major_cols(z)], axis=1).astype(BF16)


def kernel(x_prompt, x_sample, state_conv, cache_ckv, cache_kpe, cache_k, cache_v, page_table,
           w_in0, conv_w, q_norm, w_qb, kv_norm, w_uk, w_uv, w_out0, ln0_g, ln0_b,
           w_in1, w_out1, ln1_g, ln1_b):
    bsz, s_len, _ = x_prompt.shape
    dbsz, t_len, _ = x_sample.shape
    assert t_len == 1, "the sample group is one new token per sequence"
    n_pages = page_table.shape[1]
    p_len = n_pages * PAGE_SIZE
    n_phys = cache_ckv.shape[1]
    kv_w = N_KV_SB * HD_SB
    rows_p = bsz * s_len

    wa = _layout_w_in0(w_in0[0])
    wq, wqs = _layout_w_qb(w_qb[0])
    wuk_pad = _layout_w_uk(w_uk[0])
    wuv = w_uv[0].reshape(KV_RANK, H_MLA * V_DIM)
    wkv = jnp.concatenate([wuk_pad, wuv], axis=1).astype(BF16)
    wukt = wuk_pad.T.astype(BF16)
    wuv_b = wuv.astype(BF16)
    qn = q_norm[0].reshape(1, Q_RANK)
    kvn = kv_norm[0].reshape(1, KV_RANK)
    cw = conv_w[0]
    wo0 = w_out0[0].astype(BF16)
    g0, b0 = ln0_g[0].reshape(1, D_MODEL), ln0_b[0].reshape(1, D_MODEL)
    g1, b1 = ln1_g[0].reshape(1, D_MODEL), ln1_b[0].reshape(1, D_MODEL)
    w1p = w_in1[0].astype(BF16)
    w1s = _layout_w_in1_sample(w_in1[0])
    wo1 = w_out1[0].astype(BF16)
    wo1_gm = _group_major_cols(w_out1[0].T).T.astype(BF16)
    tc_p, ts_p = _rope_tables(jnp.arange(s_len))
    tc_s, ts_s = _rope_tables(jnp.full((dbsz,), p_len))

    ya, szb, q, k, v, ckv_p, kpe_p, tail = _even_in_prompt(
        x_prompt, wa, wq, wqs, wkv, qn, kvn, cw, tc_p, ts_p,
        tc_p * (MLA_SCALE * LOG2E), ts_p * (MLA_SCALE * LOG2E))
    attn = _mla_prompt(q, k, v)
    x1, q1, k1t, v1t, sz1, ka, kb, vd = _mid(
        x_prompt.reshape(rows_p, D_MODEL), ya.reshape(rows_p, D_CONV),
        attn.reshape(rows_p, H_MLA * V_DIM), szb.reshape(rows_p, H_MLA * V_DIM),
        wo0, g0, b0, w1p, s_len, BF16)
    o1 = _sb_prompt(q1.reshape(bsz, s_len, MIX1), ka.reshape(bsz, s_len, N_KV_SB * LANES),
                    kb.reshape(bsz, s_len, N_KV_SB * LANES), vd.reshape(bsz, s_len, N_KV_SB * LANES))
    y_prompt = _odd_out(x1, o1.reshape(rows_p, MIX1), sz1, wo1, g1, b1).reshape(bsz, s_len, D_MODEL)

    xs = x_sample.reshape(dbsz, D_MODEL)
    ya_s, szb_s, q_s, ckv_s, kpe_s, u_s = _even_in_sample(
        xs, wa, wq, wqs, qn, kvn, cw, tc_s, ts_s, tc_s * MLA_SCALE, ts_s * MLA_SCALE,
        state_conv[0, :, 0, :], state_conv[0, :, 1, :])
    cache_kpet = jnp.transpose(cache_kpe[0], (0, 2, 1))
    cache_kt = jnp.transpose(cache_k[0], (0, 2, 3, 1)).reshape(n_phys, kv_w, PAGE_SIZE)
    cache_vt = jnp.transpose(cache_v[0], (0, 2, 3, 1)).reshape(n_phys, kv_w, PAGE_SIZE)
    attn_s = _mla_sample(page_table, q_s, wukt, wuv_b, ckv_s, kpe_s, cache_ckv[0], cache_kpet)
    x1_s, q1_s, k1_s, v1_s, sz1_s = _mid(xs, ya_s, attn_s, szb_s, wo0, g0, b0, w1s, None, F32)
    o1_s = _sb_sample(page_table, q1_s, cache_kt, cache_vt)
    y_sample = _odd_out(x1_s, o1_s, sz1_s, wo1_gm, g1, b1).reshape(dbsz, 1, D_MODEL)

    conv_prompt = tail[:, SUBLANES - (CONV_W - 1):, :][None]
    conv_sample = jnp.stack([state_conv[0, :, 1, :], u_s], axis=1)[None]

    def from_feature_major(t):
        return jnp.transpose(t.reshape(bsz, N_KV_SB, HD_SB, s_len), (0, 3, 1, 2))[None]

    return (y_prompt, y_sample, conv_prompt, conv_sample,
            ckv_p[None], ckv_s.reshape(1, dbsz, 1, KV_RANK),
            kpe_p[None], kpe_s.reshape(1, dbsz, 1, ROPE_DIM),
            from_feature_major(k1t), k1_s.reshape(1, dbsz, 1, N_KV_SB, HD_SB),
            from_feature_major(v1t), v1_s.reshape(1, dbsz, 1, N_KV_SB, HD_SB))
```

```python
import functools
import math

import jax
import jax.numpy as jnp
from jax import lax
from jax.experimental import pallas as pl
from jax.experimental.pallas import tpu as pltpu

F32 = jnp.float32
BF16 = jnp.bfloat16

D_MODEL = 1024
D_CONV = 512
CONV_W = 3
H_MLA = 8
Q_RANK = 512
KV_RANK = 256
NOPE_DIM = 64
ROPE_DIM = 32
V_DIM = 64
ROPE_THETA = 10000.0
MLA_SCALE = (NOPE_DIM + ROPE_DIM) ** -0.5
H_SB = 16
N_KV_SB = 4
HD_SB = 64
G_SB = H_SB // N_KV_SB
SB_SCALE = HD_SB ** -0.5
MIX1 = H_SB * HD_SB
PAGE_SIZE = 128
DEPTH = 2
DEEPNORM_ALPHA = (2 * DEPTH) ** 0.25
EPS = 1e-5

LANES = 128
SUBLANES = 8
HEAD_PAD = LANES
VMEM_LIMIT = 56 * 1024 * 1024

SB_DEAD_LOG = -104.0

LOG2E = math.log2(math.e)

MLA_TQ = 1024
MLA_TK = 1024

MLA_PAGES_PER_CHUNK = 64
SB_PAGES_PER_CHUNK = 2

A_H, A_GC, A_GB, A_ZA, A_ZB, A_QA, A_KVA, A_KPE, A_KPES, A_END = (
    0, 512, 1024, 1536, 2048, 2560, 3072, 3328, 3456, 3584)


def _silu(x):
    return x * jax.nn.sigmoid(x)


def _dot(a, b):
    return jnp.dot(a, b, preferred_element_type=F32)


def _dot_nt(a, b):
    return lax.dot_general(a, b, (((1,), (1,)), ((), ())), preferred_element_type=F32)


def _rms(x, g):
    return x * lax.rsqrt(jnp.mean(jnp.square(x), axis=-1, keepdims=True) + EPS) * g


def _layer_norm(x, g, b):
    mu = jnp.mean(x, axis=-1, keepdims=True)
    xc = x - mu
    var = jnp.mean(jnp.square(xc), axis=-1, keepdims=True)
    return xc * lax.rsqrt(var + EPS) * g + b


def _even_in_common(xb, wa_ref, wq_ref, wqs_ref, qn_ref, kvn_ref, tc, ts, tcq, tsq,
                    q_ref, ckv_ref, kpe_ref, q_feature_major):
    def proj(a, b):
        return _dot(xb, wa_ref[:, a:b])

    u = proj(A_GC, A_GB) * proj(A_H, A_GC)
    qn = _rms(proj(A_QA, A_KVA), qn_ref[...]).astype(BF16)
    qf = _dot(qn, wq_ref[...])
    qs = _dot(qn, wqs_ref[...])
    for h in range(H_MLA):
        sl = slice(h * HEAD_PAD, (h + 1) * HEAD_PAD)
        qh = qf[:, sl] * tcq + qs[:, sl] * tsq
        if q_feature_major:
            q_ref[sl, :] = qh.T.astype(q_ref.dtype)
        else:
            q_ref[:, sl] = qh.astype(q_ref.dtype)
    ckv = _rms(proj(A_KVA, A_KPE), kvn_ref[...])
    ckv_ref[...] = ckv
    kpe128 = proj(A_KPE, A_KPES) * tc + proj(A_KPES, A_END) * ts
    kpe_ref[...] = kpe128[:, :ROPE_DIM]
    return u, proj(A_GB, A_ZA), proj(A_ZA, A_ZB), proj(A_ZB, A_QA), ckv, kpe128


def _even_in_prompt_kernel(x_ref, wa_ref, wq_ref, wqs_ref, wkv_ref, qn_ref, kvn_ref, cw_ref,
                           tc_ref, ts_ref, tcq_ref, tsq_ref,
                           ya_ref, szb_ref, qt_ref, k_ref, vt_ref, ckv_ref, kpe_ref, tail_ref,
                           ubuf, *, tm):
    i = pl.program_id(1)
    xb = x_ref[...].astype(BF16)
    u, gb, za, zb, ckv, kpe128 = _even_in_common(
        xb, wa_ref, wq_ref, wqs_ref, qn_ref, kvn_ref, tc_ref[...], ts_ref[...],
        tcq_ref[...], tsq_ref[...], qt_ref, ckv_ref, kpe_ref, True)

    @pl.when(i == 0)
    def _():
        ubuf[0:SUBLANES, :] = jnp.zeros((SUBLANES, D_CONV), F32)

    @pl.when(i > 0)
    def _():
        ubuf[0:SUBLANES, :] = ubuf[tm:tm + SUBLANES, :]

    ubuf[SUBLANES:tm + SUBLANES, :] = u
    cw = cw_ref[...]
    conv = (cw[0:1, :] * ubuf[SUBLANES - 2:tm + SUBLANES - 2, :]
            + cw[1:2, :] * ubuf[SUBLANES - 1:tm + SUBLANES - 1, :]
            + cw[2:3, :] * u)
    ya_ref[...] = (gb * conv * _silu(za)).astype(ya_ref.dtype)
    szb_ref[...] = _silu(zb)
    tail_ref[...] = ubuf[tm:tm + SUBLANES, :]

    ckvb = ckv.astype(BF16)
    kn = _dot(ckvb, wkv_ref[:, :H_MLA * HEAD_PAD])
    vt_ref[...] = _dot(ckvb, wkv_ref[:, H_MLA * HEAD_PAD:]).T.astype(vt_ref.dtype)
    for h in range(H_MLA):
        sl = slice(h * HEAD_PAD, (h + 1) * HEAD_PAD)
        k_ref[:, sl] = (kn[:, sl] + kpe128).astype(k_ref.dtype)


def _even_in_sample_kernel(x_ref, wa_ref, wq_ref, wqs_ref, qn_ref, kvn_ref, cw_ref,
                           tc_ref, ts_ref, tcq_ref, tsq_ref, s0_ref, s1_ref,
                           ya_ref, szb_ref, q_ref, ckv_ref, kpe_ref, u_ref):
    xb = x_ref[...].astype(BF16)
    u, gb, za, zb, _, _ = _even_in_common(
        xb, wa_ref, wq_ref, wqs_ref, qn_ref, kvn_ref, tc_ref[...], ts_ref[...],
        tcq_ref[...], tsq_ref[...], q_ref, ckv_ref, kpe_ref, False)
    cw = cw_ref[...]
    conv = cw[0:1, :] * s0_ref[...] + cw[1:2, :] * s1_ref[...] + cw[2:3, :] * u
    ya_ref[...] = (gb * conv * _silu(za)).astype(ya_ref.dtype)
    szb_ref[...] = _silu(zb)
    u_ref[...] = u


def _full(shape):
    nd = len(shape)
    return pl.BlockSpec(shape, lambda *_: (0,) * nd)


def _even_in_prompt(x, wa, wq, wqs, wkv, qn, kvn, cw, tc, ts, tcq, tsq):
    bsz, s, _ = x.shape
    tm = min(512, s)
    nt = s // tm
    row = lambda w: pl.BlockSpec((None, tm, w), lambda b, i: (b, i, 0))
    col = lambda w: pl.BlockSpec((None, w, tm), lambda b, i: (b, 0, i))
    tab = pl.BlockSpec((tm, LANES), lambda b, i: (i, 0))
    outs = (
        jax.ShapeDtypeStruct((bsz, s, D_CONV), BF16),
        jax.ShapeDtypeStruct((bsz, s, H_MLA * V_DIM), F32),
        jax.ShapeDtypeStruct((bsz, H_MLA * HEAD_PAD, s), BF16),
        jax.ShapeDtypeStruct((bsz, s, H_MLA * HEAD_PAD), BF16),
        jax.ShapeDtypeStruct((bsz, H_MLA * V_DIM, s), BF16),
        jax.ShapeDtypeStruct((bsz, s, KV_RANK), F32),
        jax.ShapeDtypeStruct((bsz, s, ROPE_DIM), F32),
        jax.ShapeDtypeStruct((bsz, SUBLANES, D_CONV), F32),
    )
    return pl.pallas_call(
        functools.partial(_even_in_prompt_kernel, tm=tm),
        grid=(bsz, nt),
        in_specs=[row(D_MODEL), _full(wa.shape), _full(wq.shape), _full(wqs.shape), _full(wkv.shape),
                  _full(qn.shape), _full(kvn.shape), _full(cw.shape), tab, tab, tab, tab],
        out_specs=[row(D_CONV), row(H_MLA * V_DIM), col(H_MLA * HEAD_PAD), row(H_MLA * HEAD_PAD),
                   col(H_MLA * V_DIM), row(KV_RANK), row(ROPE_DIM),
                   pl.BlockSpec((None, SUBLANES, D_CONV), lambda b, i: (b, 0, 0))],
        out_shape=outs,
        scratch_shapes=[pltpu.VMEM((tm + SUBLANES, D_CONV), F32)],
        compiler_params=pltpu.CompilerParams(
            dimension_semantics=("arbitrary", "arbitrary"), vmem_limit_bytes=VMEM_LIMIT),
        name="even_in_prompt",
    )(x, wa, wq, wqs, wkv, qn, kvn, cw, tc, ts, tcq, tsq)


def _even_in_sample(x, wa, wq, wqs, qn, kvn, cw, tc, ts, tcq, tsq, s0, s1):
    n = x.shape[0]
    outs = (
        jax.ShapeDtypeStruct((n, D_CONV), BF16),
        jax.ShapeDtypeStruct((n, H_MLA * V_DIM), F32),
        jax.ShapeDtypeStruct((n, H_MLA * HEAD_PAD), F32),
        jax.ShapeDtypeStruct((n, KV_RANK), F32),
        jax.ShapeDtypeStruct((n, ROPE_DIM), F32),
        jax.ShapeDtypeStruct((n, D_CONV), F32),
    )
    args = (x, wa, wq, wqs, qn, kvn, cw, tc, ts, tcq, tsq, s0, s1)
    return pl.pallas_call(
        _even_in_sample_kernel,
        grid=(1,),
        in_specs=[_full(a.shape) for a in args],
        out_specs=[_full(o.shape) for o in outs],
        out_shape=outs,
        compiler_params=pltpu.CompilerParams(vmem_limit_bytes=VMEM_LIMIT),
        name="even_in_sample",
    )(*args)


def _mla_prompt_kernel(qt_ref, k_ref, vt_ref, o_ref, *, tq, tk):
    qi = pl.program_id(2)
    key_minus_query = (lax.broadcasted_iota(jnp.int32, (tk, tq), 0)
                       - lax.broadcasted_iota(jnp.int32, (tk, tq), 1))

    def block(j, carry, masked):
        start = pl.multiple_of(j * tk, tk)
        new = []
        for e in range(2):
            m, l, acc = carry[e]
            s = _dot(k_ref[pl.ds(start, tk), e * HEAD_PAD:(e + 1) * HEAD_PAD],
                     qt_ref[e * HEAD_PAD:(e + 1) * HEAD_PAD, :])
            if masked:
                s = jnp.where(key_minus_query <= qi * tq - j * tk, s, -jnp.inf)
            m_new = jnp.maximum(m, jnp.max(s, axis=0, keepdims=True))
            p = jnp.exp2(s - m_new)
            alpha = jnp.exp2(m - m_new)
            l = alpha * l + jnp.sum(p, axis=0, keepdims=True)
            acc = alpha * acc + _dot(vt_ref[e * V_DIM:(e + 1) * V_DIM, pl.ds(start, tk)], p.astype(BF16))
            new.append((m_new, l, acc))
        return tuple(new)

    init = tuple((jnp.full((1, tq), -jnp.inf, F32), jnp.zeros((1, tq), F32),
                  jnp.zeros((V_DIM, tq), F32)) for _ in range(2))
    n_full = (qi * tq + 1) // tk
    n_end = ((qi + 1) * tq + tk - 1) // tk
    carry = lax.fori_loop(0, n_full, lambda j, c: block(j, c, False), init)
    carry = lax.fori_loop(n_full, n_end, lambda j, c: block(j, c, True), carry)
    ot = jnp.concatenate([carry[e][2] / carry[e][1] for e in range(2)], axis=0)
    o_ref[...] = ot.T


def _mla_prompt(qt, k, vt):
    bsz, s, _ = k.shape
    tq, tk = min(MLA_TQ, s), min(MLA_TK, s)
    nq = s // tq
    return pl.pallas_call(
        functools.partial(_mla_prompt_kernel, tq=tq, tk=tk),
        grid=(bsz, H_MLA // 2, nq),
        in_specs=[pl.BlockSpec((None, 2 * HEAD_PAD, tq), lambda b, h, i: (b, h, i)),
                  pl.BlockSpec((None, s, 2 * HEAD_PAD), lambda b, h, i: (b, 0, h)),
                  pl.BlockSpec((None, 2 * V_DIM, s), lambda b, h, i: (b, h, 0))],
        out_specs=pl.BlockSpec((None, tq, 2 * V_DIM), lambda b, h, i: (b, i, h)),
        out_shape=jax.ShapeDtypeStruct((bsz, s, H_MLA * V_DIM), F32),
        compiler_params=pltpu.CompilerParams(
            dimension_semantics=("arbitrary", "arbitrary", "arbitrary"), vmem_limit_bytes=VMEM_LIMIT),
        name="mla_prompt",
    )(qt, k, vt)


def _mla_sample_kernel(pt_ref, q_ref, wukt_ref, wuv_ref, ckvn_ref, kpen_ref, ckv_hbm, kpet_hbm,
                       o_ref, ckv_buf, kpe_buf, sems, *, pages_per_chunk, n_chunks):
    b = pl.program_id(0)
    nb = pl.num_programs(0)
    cp = pages_per_chunk

    def chunk_copies(seq, c, slot):
        copies = []
        for i in range(cp):
            page = pt_ref[seq, c * cp + i]
            copies.append(pltpu.make_async_copy(ckv_hbm.at[page], ckv_buf.at[slot, i], sems.at[slot, 0]))
            copies.append(pltpu.make_async_copy(kpet_hbm.at[page], kpe_buf.at[slot, i], sems.at[slot, 1]))
        return copies

    def start(seq, c, slot):
        for copy in chunk_copies(seq, c, slot):
            copy.start()

    def wait(seq, c, slot):
        for copy in chunk_copies(seq, c, slot):
            copy.wait()

    @pl.when(b == 0)
    def _():
        start(0, 0, 0)

    head_of_group = lax.broadcasted_iota(jnp.int32, (H_MLA, H_MLA * HEAD_PAD), 1) // HEAD_PAD
    row8 = lax.broadcasted_iota(jnp.int32, (H_MLA, H_MLA * HEAD_PAD), 0)
    qrow = q_ref[...]
    qbd = jnp.where(head_of_group == row8, jnp.broadcast_to(qrow, row8.shape), 0.0)
    qlat = _dot(qbd.astype(BF16), wukt_ref[...])
    q8 = jnp.zeros((H_MLA, HEAD_PAD), F32)
    sub = lax.broadcasted_iota(jnp.int32, (H_MLA, HEAD_PAD), 0)
    for h in range(H_MLA):
        q8 = jnp.where(sub == h, jnp.broadcast_to(qrow[:, h * HEAD_PAD:(h + 1) * HEAD_PAD],
                                                  (H_MLA, HEAD_PAD)), q8)
    qpe = q8[:, :ROPE_DIM]
    qlat_b = qlat.astype(BF16)
    qpe_b = qpe.astype(BF16)

    def chunk(c, carry):
        m_old, l_old, acc = carry
        slot = (b * n_chunks + c) % 2

        @pl.when(c + 1 < n_chunks)
        def _():
            start(b, c + 1, 1 - slot)

        @pl.when(jnp.logical_and(c + 1 == n_chunks, b + 1 < nb))
        def _():
            start(b + 1, 0, 1 - slot)

        wait(b, c, slot)
        ckv = ckv_buf[slot].reshape(cp * PAGE_SIZE, KV_RANK).astype(BF16)
        kpet = jnp.concatenate([kpe_buf[slot, i] for i in range(cp)], axis=1).astype(BF16)
        s = _dot_nt(qlat_b, ckv) + _dot(qpe_b, kpet)
        m_new = jnp.maximum(m_old, jnp.max(s, axis=-1, keepdims=True))
        p = jnp.exp(s - m_new)
        alpha = jnp.exp(m_old - m_new)
        l_new = alpha * l_old + jnp.sum(p, axis=-1, keepdims=True)
        acc = alpha * acc + _dot(p.astype(BF16), ckv)
        return m_new, l_new, acc

    init = (jnp.full((H_MLA, 1), -jnp.inf, F32), jnp.zeros((H_MLA, 1), F32),
            jnp.zeros((H_MLA, KV_RANK), F32))
    m_old, l_old, acc = lax.fori_loop(0, n_chunks, chunk, init)

    ckv_new = ckvn_ref[...]
    kpe_new = kpen_ref[...]
    s_new = (jnp.sum(qlat * ckv_new, axis=-1, keepdims=True)
             + jnp.sum(qpe * kpe_new, axis=-1, keepdims=True))
    m_new = jnp.maximum(m_old, s_new)
    p_new = jnp.exp(s_new - m_new)
    alpha = jnp.exp(m_old - m_new)
    l = alpha * l_old + p_new
    o_lat = (alpha * acc + p_new * ckv_new) / l
    full = _dot(o_lat.astype(BF16), wuv_ref[...])
    hv = lax.broadcasted_iota(jnp.int32, full.shape, 1) // V_DIM
    rv = lax.broadcasted_iota(jnp.int32, full.shape, 0)
    o_ref[...] = jnp.sum(jnp.where(hv == rv, full, 0.0), axis=0, keepdims=True)


def _mla_sample(page_table, q, wukt, wuv, ckv_new, kpe_new, cache_ckv, cache_kpet):
    n, n_pages = page_table.shape
    q3 = q.reshape(n, 1, H_MLA * HEAD_PAD)
    ckv3 = ckv_new.reshape(n, 1, KV_RANK)
    kpe3 = kpe_new.reshape(n, 1, ROPE_DIM)
    cp = math.gcd(MLA_PAGES_PER_CHUNK, n_pages)
    grid_spec = pltpu.PrefetchScalarGridSpec(
        num_scalar_prefetch=1,
        grid=(n,),
        in_specs=[
            pl.BlockSpec((None, 1, H_MLA * HEAD_PAD), lambda b, pt: (b, 0, 0)),
            pl.BlockSpec(wukt.shape, lambda b, pt: (0, 0)),
            pl.BlockSpec(wuv.shape, lambda b, pt: (0, 0)),
            pl.BlockSpec((None, 1, KV_RANK), lambda b, pt: (b, 0, 0)),
            pl.BlockSpec((None, 1, ROPE_DIM), lambda b, pt: (b, 0, 0)),
            pl.BlockSpec(memory_space=pl.ANY),
            pl.BlockSpec(memory_space=pl.ANY),
        ],
        out_specs=pl.BlockSpec((None, 1, H_MLA * V_DIM), lambda b, pt: (b, 0, 0)),
        scratch_shapes=[pltpu.VMEM((2, cp, PAGE_SIZE, KV_RANK), F32),
                        pltpu.VMEM((2, cp, ROPE_DIM, PAGE_SIZE), F32),
                        pltpu.SemaphoreType.DMA((2, 2))],
    )
    out = pl.pallas_call(
        functools.partial(_mla_sample_kernel, pages_per_chunk=cp, n_chunks=n_pages // cp),
        grid_spec=grid_spec,
        out_shape=jax.ShapeDtypeStruct((n, 1, H_MLA * V_DIM), F32),
        compiler_params=pltpu.CompilerParams(
            dimension_semantics=("arbitrary",), vmem_limit_bytes=VMEM_LIMIT),
        name="mla_sample",
    )(page_table, q3, wukt, wuv, ckv3, kpe3, cache_ckv, cache_kpet)
    return out.reshape(n, H_MLA * V_DIM)


def _mid_kernel(x_ref, ya_ref, attn_ref, szb_ref, wo_ref, g_ref, b_ref, w1_ref,
                x1_ref, q_ref, k_ref, v_ref, sz_ref, *sb_refs):
    yb = (attn_ref[...] * szb_ref[...]).astype(BF16)
    y = _dot(ya_ref[...], wo_ref[:D_CONV, :]) + _dot(yb, wo_ref[D_CONV:, :])
    x1 = _layer_norm(DEEPNORM_ALPHA * x_ref[...] + y, g_ref[...], b_ref[...])
    x1_ref[...] = x1
    xb = x1.astype(BF16)
    kv_w = N_KV_SB * HD_SB
    q_ref[...] = (_dot(xb, w1_ref[:, :MIX1]) * SB_SCALE).astype(q_ref.dtype)
    k = _dot(xb, w1_ref[:, MIX1:MIX1 + kv_w])
    v = _dot(xb, w1_ref[:, MIX1 + kv_w:MIX1 + 2 * kv_w])
    sz_ref[...] = _silu(_dot(xb, w1_ref[:, MIX1 + 2 * kv_w:]))
    if not sb_refs:
        k_ref[...] = k
        v_ref[...] = v
        return
    k_ref[...] = k.T
    v_ref[...] = v.T
    ka_ref, kb_ref, vd_ref = sb_refs
    low = lax.broadcasted_iota(jnp.int32, (k.shape[0], LANES), 1) < HD_SB
    for s in range(N_KV_SB // 2):
        ks = k[:, s * LANES:(s + 1) * LANES]
        vs = v[:, s * LANES:(s + 1) * LANES]
        kr = pltpu.roll(ks, HD_SB, 1)
        vr = pltpu.roll(vs, HD_SB, 1)
        even = slice(2 * s * LANES, (2 * s + 1) * LANES)
        odd = slice((2 * s + 1) * LANES, (2 * s + 2) * LANES)
        ka_ref[:, even] = jnp.where(low, ks, 0.0).astype(BF16)
        ka_ref[:, odd] = jnp.where(low, kr, 0.0).astype(BF16)
        kb_ref[:, even] = jnp.where(low, 0.0, kr).astype(BF16)
        kb_ref[:, odd] = jnp.where(low, 0.0, ks).astype(BF16)
        vd_ref[:, even] = jnp.where(low, vs, vr).astype(BF16)
        vd_ref[:, odd] = jnp.where(low, vr, vs).astype(BF16)


def _mid(x, ya, attn, szb, wo, g, b, w1, seq, q_dtype):
    n = x.shape[0]
    tm = min(512, n)
    kv_w = N_KV_SB * HD_SB
    row = lambda w: pl.BlockSpec((tm, w), lambda i: (i, 0))
    outs = [jax.ShapeDtypeStruct((n, D_MODEL), F32), jax.ShapeDtypeStruct((n, MIX1), q_dtype)]
    specs = [row(D_MODEL), row(MIX1)]
    if seq is None:
        outs += [jax.ShapeDtypeStruct((n, kv_w), F32)] * 2 + [jax.ShapeDtypeStruct((n, MIX1), F32)]
        specs += [row(kv_w), row(kv_w), row(MIX1)]
    else:
        nt = seq // tm
        feat = pl.BlockSpec((None, kv_w, tm), lambda i: (i // nt, 0, i % nt))
        outs += [jax.ShapeDtypeStruct((n // seq, kv_w, seq), F32)] * 2 + [jax.ShapeDtypeStruct((n, MIX1), F32)]
        outs += [jax.ShapeDtypeStruct((n, N_KV_SB * LANES), BF16)] * 3
        specs += [feat, feat, row(MIX1)] + [row(N_KV_SB * LANES)] * 3
    return pl.pallas_call(
        _mid_kernel,
        grid=(n // tm,),
        in_specs=[row(D_MODEL), row(D_CONV), row(H_MLA * V_DIM), row(H_MLA * V_DIM),
                  _full(wo.shape), _full(g.shape), _full(b.shape), _full(w1.shape)],
        out_specs=specs,
        out_shape=outs,
        compiler_params=pltpu.CompilerParams(
            dimension_semantics=("arbitrary",), vmem_limit_bytes=VMEM_LIMIT),
        name="even_out_odd_in",
    )(x, ya, attn, szb, wo, g, b, w1)


def _sb_logs(z):
    sp = jnp.log(1.0 + jnp.exp(-jnp.abs(z)))
    lb = jnp.minimum(z, 0.0) - sp
    return lb, lb - z


def _upper2(n):
    r = lax.broadcasted_iota(jnp.int32, (2 * n, n), 0)
    c = lax.broadcasted_iota(jnp.int32, (2 * n, n), 1)
    return jnp.where(jnp.where(r >= n, r - n, r) > c, 1.0, 0.0).astype(BF16)


def _suffix_sum_exclusive(x, upper2):
    hi = x.astype(BF16)
    lo = (x - hi.astype(F32)).astype(BF16)
    return _dot(jnp.concatenate([hi, lo], axis=1), upper2)


def _sb_prompt_kernel(q_ref, ka_ref, kb_ref, v_ref, o_ref, *, tq):
    qi = pl.program_id(2)
    upper = _upper2(tq)
    q2 = jnp.concatenate([q_ref[:, :LANES], q_ref[:, LANES:]], axis=0)
    rows = G_SB * tq
    row_in_tile = lax.rem(lax.broadcasted_iota(jnp.int32, (rows, tq), 0), tq)
    diag_mask = lax.broadcasted_iota(jnp.int32, (rows, tq), 1) < row_in_tile

    def block(j, state, masked):
        carry, acc = state
        start = pl.multiple_of(j * tq, tq)
        z = jnp.concatenate([_dot_nt(q2, ka_ref[pl.ds(start, tq), :]),
                             _dot_nt(q2, kb_ref[pl.ds(start, tq), :])], axis=0)
        lb, lk = _sb_logs(z)
        if masked:
            lk = jnp.where(diag_mask, lk, 0.0)
        later = _suffix_sum_exclusive(lk, upper)
        a = jnp.exp(lb + later + carry)
        if masked:
            a = jnp.where(diag_mask, a, 0.0)
        acc = acc + _dot(a.astype(BF16), v_ref[pl.ds(start, tq), :])
        carry = carry + later[:, 0:1] + lk[:, 0:1]
        return carry, acc

    def alive(st):
        return jnp.max(st[0]) > SB_DEAD_LOG

    def cond(ls):
        j, st, go = ls
        return jnp.logical_and(j >= 0, go)

    def body(ls):
        j, st, _ = ls
        st = block(j, st, False)
        return j - 1, st, alive(st)

    state = block(qi, (jnp.zeros((rows, 1), F32), jnp.zeros((rows, LANES), F32)), True)
    _, (_, acc), _ = lax.while_loop(cond, body, (qi - 1, state, alive(state)))
    lane = lax.broadcasted_iota(jnp.int32, (tq, LANES), 1)
    o_ref[:, :LANES] = jnp.where(lane < HD_SB, acc[0:tq], acc[2 * tq:3 * tq])
    o_ref[:, LANES:] = jnp.where(lane < HD_SB, acc[tq:2 * tq], acc[3 * tq:4 * tq])


def _sb_prompt(q, ka, kb, vdup):
    bsz, s, _ = q.shape
    tq = min(256, s)
    nq = s // tq
    kvspec = pl.BlockSpec((None, s, LANES), lambda b, j, i: (b, 0, j))
    return pl.pallas_call(
        functools.partial(_sb_prompt_kernel, tq=tq),
        grid=(bsz, N_KV_SB, nq),
        in_specs=[pl.BlockSpec((None, tq, G_SB * HD_SB), lambda b, j, i: (b, i, j)),
                  kvspec, kvspec, kvspec],
        out_specs=pl.BlockSpec((None, tq, G_SB * HD_SB), lambda b, j, i: (b, i, j)),
        out_shape=jax.ShapeDtypeStruct((bsz, s, MIX1), F32),
        compiler_params=pltpu.CompilerParams(
            dimension_semantics=("arbitrary", "arbitrary", "arbitrary"), vmem_limit_bytes=VMEM_LIMIT),
        name="sb_prompt",
    )(q, ka, kb, vdup)


def _sb_sample_kernel(pt_ref, q_ref, kt_hbm, vt_hbm, o_ref, kbuf, vbuf, sems, carry, acc,
                      *, pages_per_chunk, n_pages):
    b = pl.program_id(0)
    nb = pl.num_programs(0)
    cp = pages_per_chunk
    n_chunks = n_pages // cp
    kv_w = N_KV_SB * HD_SB
    keys = cp * PAGE_SIZE

    def chunk_copies(seq, c, slot):
        copies = []
        for i in range(cp):
            page = pt_ref[seq, n_pages - 1 - (c * cp + i)]
            copies.append(pltpu.make_async_copy(kt_hbm.at[page], kbuf.at[slot, i], sems.at[slot, 0]))
            copies.append(pltpu.make_async_copy(vt_hbm.at[page], vbuf.at[slot, i], sems.at[slot, 1]))
        return copies

    def start(seq, c, slot):
        for copy in chunk_copies(seq, c, slot):
            copy.start()

    def wait(seq, c, slot):
        for copy in chunk_copies(seq, c, slot):
            copy.wait()

    first_slot = b % 2

    @pl.when(b == 0)
    def _():
        start(0, 0, 0)

    @pl.when(b + 1 < nb)
    def _():
        start(b + 1, 0, 1 - first_slot)

    kv_of_lane = lax.broadcasted_iota(jnp.int32, (H_SB, kv_w), 1) // HD_SB
    kv_of_row = lax.broadcasted_iota(jnp.int32, (H_SB, kv_w), 0) % N_KV_SB
    own = kv_of_lane == kv_of_row
    qrow = q_ref[...]
    parts = [jnp.broadcast_to(qrow[:, g * kv_w:(g + 1) * kv_w], (N_KV_SB, kv_w)) for g in range(G_SB)]
    qbd = jnp.where(own, jnp.concatenate(parts, axis=0), 0.0).astype(BF16)
    upper = _upper2(keys)
    carry[...] = jnp.zeros(carry.shape, F32)
    acc[...] = jnp.zeros(acc.shape, F32)

    def consume(slot):
        kt = jnp.concatenate([kbuf[slot, i] for i in reversed(range(cp))], axis=1).astype(BF16)
        vt = jnp.concatenate([vbuf[slot, i] for i in reversed(range(cp))], axis=1).astype(BF16)
        z = _dot(qbd, kt)
        lb, lk = _sb_logs(z)
        later = _suffix_sum_exclusive(lk, upper)
        a = jnp.exp(lb + later + carry[...])
        acc[...] = acc[...] + _dot_nt(a.astype(BF16), vt)
        carry[...] = carry[...] + later[:, 0:1] + lk[:, 0:1]

    def alive():
        return jnp.max(carry[...]) > SB_DEAD_LOG

    wait(b, 0, first_slot)
    consume(first_slot)

    def cond(st):
        c, go = st
        return jnp.logical_and(c < n_chunks, go)

    def body(st):
        c, _ = st
        start(b, c, 2)
        wait(b, c, 2)
        consume(2)
        return c + 1, alive()

    lax.while_loop(cond, body, (jnp.int32(1), alive()))

    masked = jnp.where(own, acc[...], 0.0)
    for g in range(G_SB):
        o_ref[:, g * kv_w:(g + 1) * kv_w] = jnp.sum(
            masked[g * N_KV_SB:(g + 1) * N_KV_SB, :], axis=0, keepdims=True)


def _sb_sample(page_table, q, cache_kt, cache_vt):
    n, n_pages = page_table.shape
    kv_w = N_KV_SB * HD_SB
    cp = math.gcd(SB_PAGES_PER_CHUNK, n_pages)
    grid_spec = pltpu.PrefetchScalarGridSpec(
        num_scalar_prefetch=1,
        grid=(n,),
        in_specs=[pl.BlockSpec((None, 1, MIX1), lambda b, pt: (b, 0, 0)),
                  pl.BlockSpec(memory_space=pl.ANY), pl.BlockSpec(memory_space=pl.ANY)],
        out_specs=pl.BlockSpec((None, 1, MIX1), lambda b, pt: (b, 0, 0)),
        scratch_shapes=[pltpu.VMEM((3, cp, kv_w, PAGE_SIZE), F32),
                        pltpu.VMEM((3, cp, kv_w, PAGE_SIZE), F32),
                        pltpu.SemaphoreType.DMA((3, 2)),
                        pltpu.VMEM((H_SB, 1), F32), pltpu.VMEM((H_SB, kv_w), F32)],
    )
    out = pl.pallas_call(
        functools.partial(_sb_sample_kernel, pages_per_chunk=cp, n_pages=n_pages),
        grid_spec=grid_spec,
        out_shape=jax.ShapeDtypeStruct((n, 1, MIX1), F32),
        compiler_params=pltpu.CompilerParams(dimension_semantics=("arbitrary",)),
        name="sb_sample",
    )(page_table, q.reshape(n, 1, MIX1), cache_kt, cache_vt)
    return out.reshape(n, MIX1)


def _odd_out_kernel(x_ref, o_ref, sz_ref, wo_ref, g_ref, b_ref, y_ref):
    y = _dot((o_ref[...] * sz_ref[...]).astype(BF16), wo_ref[...])
    y_ref[...] = _layer_norm(DEEPNORM_ALPHA * x_ref[...] + y, g_ref[...], b_ref[...])


def _odd_out(x, o, sz, wo, g, b):
    n = x.shape[0]
    tm = min(512, n)
    row = pl.BlockSpec((tm, D_MODEL), lambda i: (i, 0))
    return pl.pallas_call(
        _odd_out_kernel,
        grid=(n // tm,),
        in_specs=[row, row, row, _full(wo.shape), _full(g.shape), _full(b.shape)],
        out_specs=row,
        out_shape=jax.ShapeDtypeStruct((n, D_MODEL), F32),
        compiler_params=pltpu.CompilerParams(
            dimension_semantics=("arbitrary",), vmem_limit_bytes=VMEM_LIMIT),
        name="odd_out",
    )(x, o, sz, wo, g, b)


def _swap_halves(w):
    half = w.shape[-1] // 2
    return jnp.concatenate([w[..., half:], w[..., :half]], axis=-1)


def _layout_w_in0(w):
    h, gc, gb, za, qa, kva, kpe, zb = jnp.split(
        w, [512, 1024, 1536, 2048, 2560, 2816, 2848], axis=1)
    pad = jnp.zeros((w.shape[0], HEAD_PAD - ROPE_DIM), w.dtype)
    cols = [h, gc, gb, za, zb, qa, kva, kpe, pad, _swap_halves(kpe), pad]
    return jnp.concatenate(cols, axis=1).astype(BF16)


def _layout_w_qb(w):
    w = w.reshape(Q_RANK, H_MLA, NOPE_DIM + ROPE_DIM)
    nope, pe = w[..., :NOPE_DIM], w[..., NOPE_DIM:]
    z32 = jnp.zeros((Q_RANK, H_MLA, HEAD_PAD - NOPE_DIM - ROPE_DIM), w.dtype)
    z96 = jnp.zeros((Q_RANK, H_MLA, HEAD_PAD - ROPE_DIM), w.dtype)
    wq = jnp.concatenate([pe, nope, z32], axis=-1).reshape(Q_RANK, H_MLA * HEAD_PAD)
    wqs = jnp.concatenate([_swap_halves(pe), z96], axis=-1).reshape(Q_RANK, H_MLA * HEAD_PAD)
    return wq.astype(BF16), wqs.astype(BF16)


def _layout_w_uk(w_uk):
    zl = jnp.zeros((KV_RANK, H_MLA, ROPE_DIM), w_uk.dtype)
    zr = jnp.zeros((KV_RANK, H_MLA, HEAD_PAD - NOPE_DIM - ROPE_DIM), w_uk.dtype)
    return jnp.concatenate([zl, w_uk, zr], axis=-1).reshape(KV_RANK, H_MLA * HEAD_PAD)


def _rope_tables(pos):
    half = ROPE_DIM // 2
    freq = ROPE_THETA ** (-jnp.arange(half, dtype=F32) / half)
    ang = pos.astype(F32)[:, None] * freq[None, :]
    cos, sin = jnp.cos(ang), jnp.sin(ang)
    n = pos.shape[0]
    ones = jnp.ones((n, NOPE_DIM), F32)
    zc = jnp.zeros((n, HEAD_PAD - NOPE_DIM - ROPE_DIM), F32)
    zs = jnp.zeros((n, HEAD_PAD - ROPE_DIM), F32)
    tc = jnp.concatenate([cos, cos, ones, zc], axis=1)
    ts = jnp.concatenate([-sin, sin, zs], axis=1)
    return tc, ts


def _group_major_cols(w):
    lead = w.shape[0]
    return w.reshape(lead, N_KV_SB, G_SB, HD_SB).transpose(0, 2, 1, 3).reshape(lead, MIX1)


def _layout_w_in1_sample(w):
    kv_w = N_KV_SB * HD_SB
    q, k, v, z = jnp.split(w, [MIX1, MIX1 + kv_w, MIX1 + 2 * kv_w], axis=1)
    return jnp.concatenate([_group_major_cols(q), k, v, _group_major_cols(z)], axis=1).astype(BF16)


def kernel(x_prompt, x_sample, state_conv, cache_ckv, cache_kpe, cache_k, cache_v, page_table,
           w_in0, conv_w, q_norm, w_qb, kv_norm, w_uk, w_uv, w_out0, ln0_g, ln0_b,
           w_in1, w_out1, ln1_g, ln1_b):
    bsz, s_len, _ = x_prompt.shape
    dbsz, t_len, _ = x_sample.shape
    assert t_len == 1, "the sample group is one new token per sequence"
    n_pages = page_table.shape[1]
    p_len = n_pages * PAGE_SIZE
    n_phys = cache_ckv.shape[1]
    kv_w = N_KV_SB * HD_SB
    rows_p = bsz * s_len

    wa = _layout_w_in0(w_in0[0])
    wq, wqs = _layout_w_qb(w_qb[0])
    wuk_pad = _layout_w_uk(w_uk[0])
    wuv = w_uv[0].reshape(KV_RANK, H_MLA * V_DIM)
    wkv = jnp.concatenate([wuk_pad, wuv], axis=1).astype(BF16)
    wukt = wuk_pad.T.astype(BF16)
    wuv_b = wuv.astype(BF16)
    qn = q_norm[0].reshape(1, Q_RANK)
    kvn = kv_norm[0].reshape(1, KV_RANK)
    cw = conv_w[0]
    wo0 = w_out0[0].astype(BF16)
    g0, b0 = ln0_g[0].reshape(1, D_MODEL), ln0_b[0].reshape(1, D_MODEL)
    g1, b1 = ln1_g[0].reshape(1, D_MODEL), ln1_b[0].reshape(1, D_MODEL)
    w1p = w_in1[0].astype(BF16)
    w1s = _layout_w_in1_sample(w_in1[0])
    wo1 = w_out1[0].astype(BF16)
    wo1_gm = _group_major_cols(w_out1[0].T).T.astype(BF16)
    tc_p, ts_p = _rope_tables(jnp.arange(s_len))
    tc_s, ts_s = _rope_tables(jnp.full((dbsz,), p_len))

    ya, szb, q, k, v, ckv_p, kpe_p, tail = _even_in_prompt(
        x_prompt, wa, wq, wqs, wkv, qn, kvn, cw, tc_p, ts_p,
        tc_p * (MLA_SCALE * LOG2E), ts_p * (MLA_SCALE * LOG2E))
    attn = _mla_prompt(q, k, v)
    x1, q1, k1t, v1t, sz1, ka, kb, vd = _mid(
        x_prompt.reshape(rows_p, D_MODEL), ya.reshape(rows_p, D_CONV),
        attn.reshape(rows_p, H_MLA * V_DIM), szb.reshape(rows_p, H_MLA * V_DIM),
        wo0, g0, b0, w1p, s_len, BF16)
    o1 = _sb_prompt(q1.reshape(bsz, s_len, MIX1), ka.reshape(bsz, s_len, N_KV_SB * LANES),
                    kb.reshape(bsz, s_len, N_KV_SB * LANES), vd.reshape(bsz, s_len, N_KV_SB * LANES))
    y_prompt = _odd_out(x1, o1.reshape(rows_p, MIX1), sz1, wo1, g1, b1).reshape(bsz, s_len, D_MODEL)

    xs = x_sample.reshape(dbsz, D_MODEL)
    ya_s, szb_s, q_s, ckv_s, kpe_s, u_s = _even_in_sample(
        xs, wa, wq, wqs, qn, kvn, cw, tc_s, ts_s, tc_s * MLA_SCALE, ts_s * MLA_SCALE,
        state_conv[0, :, 0, :], state_conv[0, :, 1, :])
    cache_kpet = jnp.transpose(cache_kpe[0], (0, 2, 1))
    cache_kt = jnp.transpose(cache_k[0], (0, 2, 3, 1)).reshape(n_phys, kv_w, PAGE_SIZE)
    cache_vt = jnp.transpose(cache_v[0], (0, 2, 3, 1)).reshape(n_phys, kv_w, PAGE_SIZE)
    attn_s = _mla_sample(page_table, q_s, wukt, wuv_b, ckv_s, kpe_s, cache_ckv[0], cache_kpet)
    x1_s, q1_s, k1_s, v1_s, sz1_s = _mid(xs, ya_s, attn_s, szb_s, wo0, g0, b0, w1s, None, F32)
    o1_s = _sb_sample(page_table, q1_s, cache_kt, cache_vt)
    y_sample = _odd_out(x1_s, o1_s, sz1_s, wo1_gm, g1, b1).reshape(dbsz, 1, D_MODEL)

    conv_prompt = tail[:, SUBLANES - (CONV_W - 1):, :][None]
    conv_sample = jnp.stack([state_conv[0, :, 1, :], u_s], axis=1)[None]

    def from_feature_major(t):
        return jnp.transpose(t.reshape(bsz, N_KV_SB, HD_SB, s_len), (0, 3, 1, 2))[None]

    return (y_prompt, y_sample, conv_prompt, conv_sample,
            ckv_p[None], ckv_s.reshape(1, dbsz, 1, KV_RANK),
            kpe_p[None], kpe_s.reshape(1, dbsz, 1, ROPE_DIM),
            from_feature_major(k1t), k1_s.reshape(1, dbsz, 1, N_KV_SB, HD_SB),
            from_feature_major(v1t), v1_s.reshape(1, dbsz, 1, N_KV_SB, HD_SB))
```

```python
import functools
import math

import jax
import jax.numpy as jnp
from jax import lax
from jax.experimental import pallas as pl
from jax.experimental.pallas import tpu as pltpu

F32 = jnp.float32
BF16 = jnp.bfloat16

D_MODEL = 1024
D_CONV = 512
CONV_W = 3
H_MLA = 8
Q_RANK = 512
KV_RANK = 256
NOPE_DIM = 64
ROPE_DIM = 32
V_DIM = 64
ROPE_THETA = 10000.0
MLA_SCALE = (NOPE_DIM + ROPE_DIM) ** -0.5
H_SB = 16
N_KV_SB = 4
HD_SB = 64
G_SB = H_SB // N_KV_SB
SB_SCALE = HD_SB ** -0.5
MIX1 = H_SB * HD_SB
PAGE_SIZE = 128
DEPTH = 2
DEEPNORM_ALPHA = (2 * DEPTH) ** 0.25
EPS = 1e-5

LANES = 128
SUBLANES = 8
HEAD_PAD = LANES
VMEM_LIMIT = 56 * 1024 * 1024

SB_DEAD_LOG = -104.0

LOG2E = math.log2(math.e)

MLA_TQ = 1024
MLA_TK = 1024

MLA_PAGES_PER_CHUNK = 64
SB_PAGES_PER_CHUNK = 2

A_H, A_GC, A_GB, A_ZA, A_ZB, A_QA, A_KVA, A_KPE, A_KPES, A_END = (
    0, 512, 1024, 1536, 2048, 2560, 3072, 3328, 3456, 3584)


def _silu(x):
    return x * jax.nn.sigmoid(x)


def _dot(a, b):
    return jnp.dot(a, b, preferred_element_type=F32)


def _dot_nt(a, b):
    return lax.dot_general(a, b, (((1,), (1,)), ((), ())), preferred_element_type=F32)


def _rms(x, g):
    return x * lax.rsqrt(jnp.mean(jnp.square(x), axis=-1, keepdims=True) + EPS) * g


def _layer_norm(x, g, b):
    mu = jnp.mean(x, axis=-1, keepdims=True)
    xc = x - mu
    var = jnp.mean(jnp.square(xc), axis=-1, keepdims=True)
    return xc * lax.rsqrt(var + EPS) * g + b


def _even_in_common(xb, wa_ref, wq_ref, wqs_ref, qn_ref, kvn_ref, tc, ts, tcq, tsq,
                    q_ref, ckv_ref, kpe_ref, q_feature_major):
    def proj(a, b):
        return _dot(xb, wa_ref[:, a:b])

    u = proj(A_GC, A_GB) * proj(A_H, A_GC)
    qn = _rms(proj(A_QA, A_KVA), qn_ref[...]).astype(BF16)
    qf = _dot(qn, wq_ref[...])
    qs = _dot(qn, wqs_ref[...])
    for h in range(H_MLA):
        sl = slice(h * HEAD_PAD, (h + 1) * HEAD_PAD)
        qh = qf[:, sl] * tcq + qs[:, sl] * tsq
        if q_feature_major:
            q_ref[sl, :] = qh.T.astype(q_ref.dtype)
        else:
            q_ref[:, sl] = qh.astype(q_ref.dtype)
    ckv = _rms(proj(A_KVA, A_KPE), kvn_ref[...])
    ckv_ref[...] = ckv
    kpe128 = proj(A_KPE, A_KPES) * tc + proj(A_KPES, A_END) * ts
    kpe_ref[...] = kpe128[:, :ROPE_DIM]
    return u, proj(A_GB, A_ZA), proj(A_ZA, A_ZB), proj(A_ZB, A_QA), ckv, kpe128


def _even_in_prompt_kernel(x_ref, wa_ref, wq_ref, wqs_ref, wkv_ref, qn_ref, kvn_ref, cw_ref,
                           tc_ref, ts_ref, tcq_ref, tsq_ref,
                           ya_ref, szb_ref, qt_ref, k_ref, vt_ref, ckv_ref, kpe_ref, tail_ref,
                           ubuf, *, tm):
    i = pl.program_id(1)
    xb = x_ref[...].astype(BF16)
    u, gb, za, zb, ckv, kpe128 = _even_in_common(
        xb, wa_ref, wq_ref, wqs_ref, qn_ref, kvn_ref, tc_ref[...], ts_ref[...],
        tcq_ref[...], tsq_ref[...], qt_ref, ckv_ref, kpe_ref, True)

    @pl.when(i == 0)
    def _():
        ubuf[0:SUBLANES, :] = jnp.zeros((SUBLANES, D_CONV), F32)

    @pl.when(i > 0)
    def _():
        ubuf[0:SUBLANES, :] = ubuf[tm:tm + SUBLANES, :]

    ubuf[SUBLANES:tm + SUBLANES, :] = u
    cw = cw_ref[...]
    conv = (cw[0:1, :] * ubuf[SUBLANES - 2:tm + SUBLANES - 2, :]
            + cw[1:2, :] * ubuf[SUBLANES - 1:tm + SUBLANES - 1, :]
            + cw[2:3, :] * u)
    ya_ref[...] = (gb * conv * _silu(za)).astype(ya_ref.dtype)
    szb_ref[...] = _silu(zb)
    tail_ref[...] = ubuf[tm:tm + SUBLANES, :]

    ckvb = ckv.astype(BF16)
    kn = _dot(ckvb, wkv_ref[:, :H_MLA * HEAD_PAD])
    vt_ref[...] = _dot(ckvb, wkv_ref[:, H_MLA * HEAD_PAD:]).T.astype(vt_ref.dtype)
    for h in range(H_MLA):
        sl = slice(h * HEAD_PAD, (h + 1) * HEAD_PAD)
        k_ref[:, sl] = (kn[:, sl] + kpe128).astype(k_ref.dtype)


def _even_in_sample_kernel(x_ref, wa_ref, wq_ref, wqs_ref, qn_ref, kvn_ref, cw_ref,
                           tc_ref, ts_ref, tcq_ref, tsq_ref, s0_ref, s1_ref,
                           ya_ref, szb_ref, q_ref, ckv_ref, kpe_ref, u_ref):
    xb = x_ref[...].astype(BF16)
    u, gb, za, zb, _, _ = _even_in_common(
        xb, wa_ref, wq_ref, wqs_ref, qn_ref, kvn_ref, tc_ref[...], ts_ref[...],
        tcq_ref[...], tsq_ref[...], q_ref, ckv_ref, kpe_ref, False)
    cw = cw_ref[...]
    conv = cw[0:1, :] * s0_ref[...] + cw[1:2, :] * s1_ref[...] + cw[2:3, :] * u
    ya_ref[...] = (gb * conv * _silu(za)).astype(ya_ref.dtype)
    szb_ref[...] = _silu(zb)
    u_ref[...] = u


def _full(shape):
    nd = len(shape)
    return pl.BlockSpec(shape, lambda *_: (0,) * nd)


def _even_in_prompt(x, wa, wq, wqs, wkv, qn, kvn, cw, tc, ts, tcq, tsq):
    bsz, s, _ = x.shape
    tm = min(512, s)
    nt = s // tm
    row = lambda w: pl.BlockSpec((None, tm, w), lambda b, i: (b, i, 0))
    col = lambda w: pl.BlockSpec((None, w, tm), lambda b, i: (b, 0, i))
    tab = pl.BlockSpec((tm, LANES), lambda b, i: (i, 0))
    outs = (
        jax.ShapeDtypeStruct((bsz, s, D_CONV), BF16),
        jax.ShapeDtypeStruct((bsz, s, H_MLA * V_DIM), F32),
        jax.ShapeDtypeStruct((bsz, H_MLA * HEAD_PAD, s), BF16),
        jax.ShapeDtypeStruct((bsz, s, H_MLA * HEAD_PAD), BF16),
        jax.ShapeDtypeStruct((bsz, H_MLA * V_DIM, s), BF16),
        jax.ShapeDtypeStruct((bsz, s, KV_RANK), F32),
        jax.ShapeDtypeStruct((bsz, s, ROPE_DIM), F32),
        jax.ShapeDtypeStruct((bsz, SUBLANES, D_CONV), F32),
    )
    return pl.pallas_call(
        functools.partial(_even_in_prompt_kernel, tm=tm),
        grid=(bsz, nt),
        in_specs=[row(D_MODEL), _full(wa.shape), _full(wq.shape), _full(wqs.shape), _full(wkv.shape),
                  _full(qn.shape), _full(kvn.shape), _full(cw.shape), tab, tab, tab, tab],
        out_specs=[row(D_CONV), row(H_MLA * V_DIM), col(H_MLA * HEAD_PAD), row(H_MLA * HEAD_PAD),
                   col(H_MLA * V_DIM), row(KV_RANK), row(ROPE_DIM),
                   pl.BlockSpec((None, SUBLANES, D_CONV), lambda b, i: (b, 0, 0))],
        out_shape=outs,
        scratch_shapes=[pltpu.VMEM((tm + SUBLANES, D_CONV), F32)],
        compiler_params=pltpu.CompilerParams(
            dimension_semantics=("arbitrary", "arbitrary"), vmem_limit_bytes=VMEM_LIMIT),
        name="even_in_prompt",
    )(x, wa, wq, wqs, wkv, qn, kvn, cw, tc, ts, tcq, tsq)


def _even_in_sample(x, wa, wq, wqs, qn, kvn, cw, tc, ts, tcq, tsq, s0, s1):
    n = x.shape[0]
    outs = (
        jax.ShapeDtypeStruct((n, D_CONV), BF16),
        jax.ShapeDtypeStruct((n, H_MLA * V_DIM), F32),
        jax.ShapeDtypeStruct((n, H_MLA * HEAD_PAD), F32),
        jax.ShapeDtypeStruct((n, KV_RANK), F32),
        jax.ShapeDtypeStruct((n, ROPE_DIM), F32),
        jax.ShapeDtypeStruct((n, D_CONV), F32),
    )
    args = (x, wa, wq, wqs, qn, kvn, cw, tc, ts, tcq, tsq, s0, s1)
    return pl.pallas_call(
        _even_in_sample_kernel,
        grid=(1,),
        in_specs=[_full(a.shape) for a in args],
        out_specs=[_full(o.shape) for o in outs],
        out_shape=outs,
        compiler_params=pltpu.CompilerParams(vmem_limit_bytes=VMEM_LIMIT),
        name="even_in_sample",
    )(*args)


def _mla_prompt_kernel(qt_ref, k_ref, vt_ref, o_ref, *, tq, tk):
    qi = pl.program_id(2)
    key_minus_query = (lax.broadcasted_iota(jnp.int32, (tk, tq), 0)
                       - lax.broadcasted_iota(jnp.int32, (tk, tq), 1))

    def block(j, carry, masked):
        start = pl.multiple_of(j * tk, tk)
        new = []
        for e in range(2):
            m, l, acc = carry[e]
            s = _dot(k_ref[pl.ds(start, tk), e * HEAD_PAD:(e + 1) * HEAD_PAD],
                     qt_ref[e * HEAD_PAD:(e + 1) * HEAD_PAD, :])
            if masked:
                s = jnp.where(key_minus_query <= qi * tq - j * tk, s, -jnp.inf)
            m_new = jnp.maximum(m, jnp.max(s, axis=0, keepdims=True))
            p = jnp.exp2(s - m_new)
            alpha = jnp.exp2(m - m_new)
            l = alpha * l + jnp.sum(p, axis=0, keepdims=True)
            acc = alpha * acc + _dot(vt_ref[e * V_DIM:(e + 1) * V_DIM, pl.ds(start, tk)], p.astype(BF16))
            new.append((m_new, l, acc))
        return tuple(new)

    init = tuple((jnp.full((1, tq), -jnp.inf, F32), jnp.zeros((1, tq), F32),
                  jnp.zeros((V_DIM, tq), F32)) for _ in range(2))
    n_full = (qi * tq + 1) // tk
    n_end = ((qi + 1) * tq + tk - 1) // tk
    carry = lax.fori_loop(0, n_full, lambda j, c: block(j, c, False), init)
    carry = lax.fori_loop(n_full, n_end, lambda j, c: block(j, c, True), carry)
    ot = jnp.concatenate([carry[e][2] / carry[e][1] for e in range(2)], axis=0)
    o_ref[...] = ot.T


def _mla_prompt(qt, k, vt):
    bsz, s, _ = k.shape
    tq, tk = min(MLA_TQ, s), min(MLA_TK, s)
    nq = s // tq
    return pl.pallas_call(
        functools.partial(_mla_prompt_kernel, tq=tq, tk=tk),
        grid=(bsz, H_MLA // 2, nq),
        in_specs=[pl.BlockSpec((None, 2 * HEAD_PAD, tq), lambda b, h, i: (b, h, i)),
                  pl.BlockSpec((None, s, 2 * HEAD_PAD), lambda b, h, i: (b, 0, h)),
                  pl.BlockSpec((None, 2 * V_DIM, s), lambda b, h, i: (b, h, 0))],
        out_specs=pl.BlockSpec((None, tq, 2 * V_DIM), lambda b, h, i: (b, i, h)),
        out_shape=jax.ShapeDtypeStruct((bsz, s, H_MLA * V_DIM), F32),
        compiler_params=pltpu.CompilerParams(
            dimension_semantics=("arbitrary", "arbitrary", "arbitrary"), vmem_limit_bytes=VMEM_LIMIT),
        name="mla_prompt",
    )(qt, k, vt)


def _mla_sample_kernel(pt_ref, q_ref, wukt_ref, wuv_ref, ckvn_ref, kpen_ref, ckv_hbm, kpet_hbm,
                       o_ref, ckv_buf, kpe_buf, sems, *, pages_per_chunk, n_chunks):
    b = pl.program_id(0)
    nb = pl.num_programs(0)
    cp = pages_per_chunk

    def chunk_copies(seq, c, slot):
        copies = []
        for i in range(cp):
            page = pt_ref[seq, c * cp + i]
            copies.append(pltpu.make_async_copy(ckv_hbm.at[page], ckv_buf.at[slot, i], sems.at[slot, 0]))
            copies.append(pltpu.make_async_copy(kpet_hbm.at[page], kpe_buf.at[slot, i], sems.at[slot, 1]))
        return copies

    def start(seq, c, slot):
        for copy in chunk_copies(seq, c, slot):
            copy.start()

    def wait(seq, c, slot):
        for copy in chunk_copies(seq, c, slot):
            copy.wait()

    @pl.when(b == 0)
    def _():
        start(0, 0, 0)

    head_of_group = lax.broadcasted_iota(jnp.int32, (H_MLA, H_MLA * HEAD_PAD), 1) // HEAD_PAD
    row8 = lax.broadcasted_iota(jnp.int32, (H_MLA, H_MLA * HEAD_PAD), 0)
    qrow = q_ref[...]
    qbd = jnp.where(head_of_group == row8, jnp.broadcast_to(qrow, row8.shape), 0.0)
    qlat = _dot(qbd.astype(BF16), wukt_ref[...])
    q8 = jnp.zeros((H_MLA, HEAD_PAD), F32)
    sub = lax.broadcasted_iota(jnp.int32, (H_MLA, HEAD_PAD), 0)
    for h in range(H_MLA):
        q8 = jnp.where(sub == h, jnp.broadcast_to(qrow[:, h * HEAD_PAD:(h + 1) * HEAD_PAD],
                                                  (H_MLA, HEAD_PAD)), q8)
    qpe = q8[:, :ROPE_DIM]
    qlat_b = qlat.astype(BF16)
    qpe_b = qpe.astype(BF16)

    def chunk(c, carry):
        m_old, l_old, acc = carry
        slot = (b * n_chunks + c) % 2

        @pl.when(c + 1 < n_chunks)
        def _():
            start(b, c + 1, 1 - slot)

        @pl.when(jnp.logical_and(c + 1 == n_chunks, b + 1 < nb))
        def _():
            start(b + 1, 0, 1 - slot)

        wait(b, c, slot)
        ckv = ckv_buf[slot].reshape(cp * PAGE_SIZE, KV_RANK).astype(BF16)
        kpet = jnp.concatenate([kpe_buf[slot, i] for i in range(cp)], axis=1).astype(BF16)
        s = _dot_nt(qlat_b, ckv) + _dot(qpe_b, kpet)
        m_new = jnp.maximum(m_old, jnp.max(s, axis=-1, keepdims=True))
        p = jnp.exp(s - m_new)
        alpha = jnp.exp(m_old - m_new)
        l_new = alpha * l_old + jnp.sum(p, axis=-1, keepdims=True)
        acc = alpha * acc + _dot(p.astype(BF16), ckv)
        return m_new, l_new, acc

    init = (jnp.full((H_MLA, 1), -jnp.inf, F32), jnp.zeros((H_MLA, 1), F32),
            jnp.zeros((H_MLA, KV_RANK), F32))
    m_old, l_old, acc = lax.fori_loop(0, n_chunks, chunk, init)

    ckv_new = ckvn_ref[...]
    kpe_new = kpen_ref[...]
    s_new = (jnp.sum(qlat * ckv_new, axis=-1, keepdims=True)
             + jnp.sum(qpe * kpe_new, axis=-1, keepdims=True))
    m_new = jnp.maximum(m_old, s_new)
    p_new = jnp.exp(s_new - m_new)
    alpha = jnp.exp(m_old - m_new)
    l = alpha * l_old + p_new
    o_lat = (alpha * acc + p_new * ckv_new) / l
    full = _dot(o_lat.astype(BF16), wuv_ref[...])
    hv = lax.broadcasted_iota(jnp.int32, full.shape, 1) // V_DIM
    rv = lax.broadcasted_iota(jnp.int32, full.shape, 0)
    o_ref[...] = jnp.sum(jnp.where(hv == rv, full, 0.0), axis=0, keepdims=True)


def _mla_sample(page_table, q, wukt, wuv, ckv_new, kpe_new, cache_ckv, cache_kpet):
    n, n_pages = page_table.shape
    q3 = q.reshape(n, 1, H_MLA * HEAD_PAD)
    ckv3 = ckv_new.reshape(n, 1, KV_RANK)
    kpe3 = kpe_new.reshape(n, 1, ROPE_DIM)
    cp = math.gcd(MLA_PAGES_PER_CHUNK, n_pages)
    grid_spec = pltpu.PrefetchScalarGridSpec(
        num_scalar_prefetch=1,
        grid=(n,),
        in_specs=[
            pl.BlockSpec((None, 1, H_MLA * HEAD_PAD), lambda b, pt: (b, 0, 0)),
            pl.BlockSpec(wukt.shape, lambda b, pt: (0, 0)),
            pl.BlockSpec(wuv.shape, lambda b, pt: (0, 0)),
            pl.BlockSpec((None, 1, KV_RANK), lambda b, pt: (b, 0, 0)),
            pl.BlockSpec((None, 1, ROPE_DIM), lambda b, pt: (b, 0, 0)),
            pl.BlockSpec(memory_space=pl.ANY),
            pl.BlockSpec(memory_space=pl.ANY),
        ],
        out_specs=pl.BlockSpec((None, 1, H_MLA * V_DIM), lambda b, pt: (b, 0, 0)),
        scratch_shapes=[pltpu.VMEM((2, cp, PAGE_SIZE, KV_RANK), F32),
                        pltpu.VMEM((2, cp, ROPE_DIM, PAGE_SIZE), F32),
                        pltpu.SemaphoreType.DMA((2, 2))],
    )
    out = pl.pallas_call(
        functools.partial(_mla_sample_kernel, pages_per_chunk=cp, n_chunks=n_pages // cp),
        grid_spec=grid_spec,
        out_shape=jax.ShapeDtypeStruct((n, 1, H_MLA * V_DIM), F32),
        compiler_params=pltpu.CompilerParams(
            dimension_semantics=("arbitrary",), vmem_limit_bytes=VMEM_LIMIT),
        name="mla_sample",
    )(page_table, q3, wukt, wuv, ckv3, kpe3, cache_ckv, cache_kpet)
    return out.reshape(n, H_MLA * V_DIM)


def _mid_kernel(x_ref, ya_ref, attn_ref, szb_ref, wo_ref, g_ref, b_ref, w1_ref,
                x1_ref, q_ref, k_ref, v_ref, sz_ref, *sb_refs):
    yb = (attn_ref[...] * szb_ref[...]).astype(BF16)
    y = _dot(ya_ref[...], wo_ref[:D_CONV, :]) + _dot(yb, wo_ref[D_CONV:, :])
    x1 = _layer_norm(DEEPNORM_ALPHA * x_ref[...] + y, g_ref[...], b_ref[...])
    x1_ref[...] = x1
    xb = x1.astype(BF16)
    kv_w = N_KV_SB * HD_SB
    q_ref[...] = (_dot(xb, w1_ref[:, :MIX1]) * SB_SCALE).astype(q_ref.dtype)
    k = _dot(xb, w1_ref[:, MIX1:MIX1 + kv_w])
    v = _dot(xb, w1_ref[:, MIX1 + kv_w:MIX1 + 2 * kv_w])
    sz_ref[...] = _silu(_dot(xb, w1_ref[:, MIX1 + 2 * kv_w:])).astype(sz_ref.dtype)
    if not sb_refs:
        k_ref[...] = k
        v_ref[...] = v
        return
    k_ref[...] = k.T
    v_ref[...] = v.T
    ka_ref, kb_ref, vd_ref = sb_refs
    low = lax.broadcasted_iota(jnp.int32, (k.shape[0], LANES), 1) < HD_SB
    for s in range(N_KV_SB // 2):
        ks = k[:, s * LANES:(s + 1) * LANES]
        vs = v[:, s * LANES:(s + 1) * LANES]
        kr = pltpu.roll(ks, HD_SB, 1)
        vr = pltpu.roll(vs, HD_SB, 1)
        even = slice(2 * s * LANES, (2 * s + 1) * LANES)
        odd = slice((2 * s + 1) * LANES, (2 * s + 2) * LANES)
        ka_ref[:, even] = jnp.where(low, ks, 0.0).astype(BF16)
        ka_ref[:, odd] = jnp.where(low, kr, 0.0).astype(BF16)
        kb_ref[:, even] = jnp.where(low, 0.0, kr).astype(BF16)
        kb_ref[:, odd] = jnp.where(low, 0.0, ks).astype(BF16)
        vd_ref[:, even] = jnp.where(low, vs, vr).astype(BF16)
        vd_ref[:, odd] = jnp.where(low, vr, vs).astype(BF16)


def _mid(x, ya, attn, szb, wo, g, b, w1, seq, q_dtype):
    n = x.shape[0]
    tm = min(512, n)
    kv_w = N_KV_SB * HD_SB
    row = lambda w: pl.BlockSpec((tm, w), lambda i: (i, 0))
    outs = [jax.ShapeDtypeStruct((n, D_MODEL), F32), jax.ShapeDtypeStruct((n, MIX1), q_dtype)]
    specs = [row(D_MODEL), row(MIX1)]
    if seq is None:
        outs += [jax.ShapeDtypeStruct((n, kv_w), F32)] * 2 + [jax.ShapeDtypeStruct((n, MIX1), F32)]
        specs += [row(kv_w), row(kv_w), row(MIX1)]
    else:
        nt = seq // tm
        feat = pl.BlockSpec((None, kv_w, tm), lambda i: (i // nt, 0, i % nt))
        outs += [jax.ShapeDtypeStruct((n // seq, kv_w, seq), F32)] * 2 + [jax.ShapeDtypeStruct((n, MIX1), BF16)]
        outs += [jax.ShapeDtypeStruct((n, N_KV_SB * LANES), BF16)] * 3
        specs += [feat, feat, row(MIX1)] + [row(N_KV_SB * LANES)] * 3
    return pl.pallas_call(
        _mid_kernel,
        grid=(n // tm,),
        in_specs=[row(D_MODEL), row(D_CONV), row(H_MLA * V_DIM), row(H_MLA * V_DIM),
                  _full(wo.shape), _full(g.shape), _full(b.shape), _full(w1.shape)],
        out_specs=specs,
        out_shape=outs,
        compiler_params=pltpu.CompilerParams(
            dimension_semantics=("arbitrary",), vmem_limit_bytes=VMEM_LIMIT),
        name="even_out_odd_in",
    )(x, ya, attn, szb, wo, g, b, w1)


def _sb_logs(z):
    sp = jnp.log(1.0 + jnp.exp(-jnp.abs(z)))
    lb = jnp.minimum(z, 0.0) - sp
    return lb, lb - z


def _upper2(n):
    r = lax.broadcasted_iota(jnp.int32, (2 * n, n), 0)
    c = lax.broadcasted_iota(jnp.int32, (2 * n, n), 1)
    return jnp.where(jnp.where(r >= n, r - n, r) > c, 1.0, 0.0).astype(BF16)


def _suffix_sum_exclusive(x, upper2):
    hi = x.astype(BF16)
    lo = (x - hi.astype(F32)).astype(BF16)
    return _dot(jnp.concatenate([hi, lo], axis=1), upper2)


def _sb_prompt_kernel(q_ref, ka_ref, kb_ref, v_ref, o_ref, *, tq):
    qi = pl.program_id(2)
    upper = _upper2(tq)
    q2 = jnp.concatenate([q_ref[:, :LANES], q_ref[:, LANES:]], axis=0)
    rows = G_SB * tq
    row_in_tile = lax.rem(lax.broadcasted_iota(jnp.int32, (rows, tq), 0), tq)
    diag_mask = lax.broadcasted_iota(jnp.int32, (rows, tq), 1) < row_in_tile

    def block(j, state, masked):
        carry, acc = state
        start = pl.multiple_of(j * tq, tq)
        z = jnp.concatenate([_dot_nt(q2, ka_ref[pl.ds(start, tq), :]),
                             _dot_nt(q2, kb_ref[pl.ds(start, tq), :])], axis=0)
        lb, lk = _sb_logs(z)
        if masked:
            lk = jnp.where(diag_mask, lk, 0.0)
        later = _suffix_sum_exclusive(lk, upper)
        a = jnp.exp(lb + later + carry)
        if masked:
            a = jnp.where(diag_mask, a, 0.0)
        acc = acc + _dot(a.astype(BF16), v_ref[pl.ds(start, tq), :])
        carry = carry + later[:, 0:1] + lk[:, 0:1]
        return carry, acc

    def alive(st):
        return jnp.max(st[0]) > SB_DEAD_LOG

    def cond(ls):
        j, st, go = ls
        return jnp.logical_and(j >= 0, go)

    def body(ls):
        j, st, _ = ls
        st = block(j, st, False)
        return j - 1, st, alive(st)

    state = block(qi, (jnp.zeros((rows, 1), F32), jnp.zeros((rows, LANES), F32)), True)
    _, (_, acc), _ = lax.while_loop(cond, body, (qi - 1, state, alive(state)))
    lane = lax.broadcasted_iota(jnp.int32, (tq, LANES), 1)
    o_ref[:, :LANES] = jnp.where(lane < HD_SB, acc[0:tq], acc[2 * tq:3 * tq]).astype(o_ref.dtype)
    o_ref[:, LANES:] = jnp.where(lane < HD_SB, acc[tq:2 * tq], acc[3 * tq:4 * tq]).astype(o_ref.dtype)


def _sb_prompt(q, ka, kb, vdup):
    bsz, s, _ = q.shape
    tq = min(256, s)
    nq = s // tq
    kvspec = pl.BlockSpec((None, s, LANES), lambda b, j, i: (b, 0, j))
    return pl.pallas_call(
        functools.partial(_sb_prompt_kernel, tq=tq),
        grid=(bsz, N_KV_SB, nq),
        in_specs=[pl.BlockSpec((None, tq, G_SB * HD_SB), lambda b, j, i: (b, i, j)),
                  kvspec, kvspec, kvspec],
        out_specs=pl.BlockSpec((None, tq, G_SB * HD_SB), lambda b, j, i: (b, i, j)),
        out_shape=jax.ShapeDtypeStruct((bsz, s, MIX1), BF16),
        compiler_params=pltpu.CompilerParams(
            dimension_semantics=("arbitrary", "arbitrary", "arbitrary"), vmem_limit_bytes=VMEM_LIMIT),
        name="sb_prompt",
    )(q, ka, kb, vdup)


def _sb_sample_kernel(pt_ref, q_ref, kt_hbm, vt_hbm, o_ref, kbuf, vbuf, sems, carry, acc,
                      *, pages_per_chunk, n_pages):
    b = pl.program_id(0)
    nb = pl.num_programs(0)
    cp = pages_per_chunk
    n_chunks = n_pages // cp
    kv_w = N_KV_SB * HD_SB
    keys = cp * PAGE_SIZE

    def chunk_copies(seq, c, slot):
        copies = []
        for i in range(cp):
            page = pt_ref[seq, n_pages - 1 - (c * cp + i)]
            copies.append(pltpu.make_async_copy(kt_hbm.at[page], kbuf.at[slot, i], sems.at[slot, 0]))
            copies.append(pltpu.make_async_copy(vt_hbm.at[page], vbuf.at[slot, i], sems.at[slot, 1]))
        return copies

    def start(seq, c, slot):
        for copy in chunk_copies(seq, c, slot):
            copy.start()

    def wait(seq, c, slot):
        for copy in chunk_copies(seq, c, slot):
            copy.wait()

    first_slot = b % 2

    @pl.when(b == 0)
    def _():
        start(0, 0, 0)

    @pl.when(b + 1 < nb)
    def _():
        start(b + 1, 0, 1 - first_slot)

    kv_of_lane = lax.broadcasted_iota(jnp.int32, (H_SB, kv_w), 1) // HD_SB
    kv_of_row = lax.broadcasted_iota(jnp.int32, (H_SB, kv_w), 0) % N_KV_SB
    own = kv_of_lane == kv_of_row
    qrow = q_ref[...]
    parts = [jnp.broadcast_to(qrow[:, g * kv_w:(g + 1) * kv_w], (N_KV_SB, kv_w)) for g in range(G_SB)]
    qbd = jnp.where(own, jnp.concatenate(parts, axis=0), 0.0).astype(BF16)
    upper = _upper2(keys)
    carry[...] = jnp.zeros(carry.shape, F32)
    acc[...] = jnp.zeros(acc.shape, F32)

    def consume(slot):
        kt = jnp.concatenate([kbuf[slot, i] for i in reversed(range(cp))], axis=1).astype(BF16)
        vt = jnp.concatenate([vbuf[slot, i] for i in reversed(range(cp))], axis=1).astype(BF16)
        z = _dot(qbd, kt)
        lb, lk = _sb_logs(z)
        later = _suffix_sum_exclusive(lk, upper)
        a = jnp.exp(lb + later + carry[...])
        acc[...] = acc[...] + _dot_nt(a.astype(BF16), vt)
        carry[...] = carry[...] + later[:, 0:1] + lk[:, 0:1]

    def alive():
        return jnp.max(carry[...]) > SB_DEAD_LOG

    wait(b, 0, first_slot)
    consume(first_slot)

    def cond(st):
        c, go = st
        return jnp.logical_and(c < n_chunks, go)

    def body(st):
        c, _ = st
        start(b, c, 2)
        wait(b, c, 2)
        consume(2)
        return c + 1, alive()

    lax.while_loop(cond, body, (jnp.int32(1), alive()))

    masked = jnp.where(own, acc[...], 0.0)
    for g in range(G_SB):
        o_ref[:, g * kv_w:(g + 1) * kv_w] = jnp.sum(
            masked[g * N_KV_SB:(g + 1) * N_KV_SB, :], axis=0, keepdims=True)


def _sb_sample(page_table, q, cache_kt, cache_vt):
    n, n_pages = page_table.shape
    kv_w = N_KV_SB * HD_SB
    cp = math.gcd(SB_PAGES_PER_CHUNK, n_pages)
    grid_spec = pltpu.PrefetchScalarGridSpec(
        num_scalar_prefetch=1,
        grid=(n,),
        in_specs=[pl.BlockSpec((None, 1, MIX1), lambda b, pt: (b, 0, 0)),
                  pl.BlockSpec(memory_space=pl.ANY), pl.BlockSpec(memory_space=pl.ANY)],
        out_specs=pl.BlockSpec((None, 1, MIX1), lambda b, pt: (b, 0, 0)),
        scratch_shapes=[pltpu.VMEM((3, cp, kv_w, PAGE_SIZE), F32),
                        pltpu.VMEM((3, cp, kv_w, PAGE_SIZE), F32),
                        pltpu.SemaphoreType.DMA((3, 2)),
                        pltpu.VMEM((H_SB, 1), F32), pltpu.VMEM((H_SB, kv_w), F32)],
    )
    out = pl.pallas_call(
        functools.partial(_sb_sample_kernel, pages_per_chunk=cp, n_pages=n_pages),
        grid_spec=grid_spec,
        out_shape=jax.ShapeDtypeStruct((n, 1, MIX1), F32),
        compiler_params=pltpu.CompilerParams(dimension_semantics=("arbitrary",)),
        name="sb_sample",
    )(page_table, q.reshape(n, 1, MIX1), cache_kt, cache_vt)
    return out.reshape(n, MIX1)


def _odd_out_kernel(x_ref, o_ref, sz_ref, wo_ref, g_ref, b_ref, y_ref):
    y = _dot((o_ref[...] * sz_ref[...]).astype(BF16), wo_ref[...])
    y_ref[...] = _layer_norm(DEEPNORM_ALPHA * x_ref[...] + y, g_ref[...], b_ref[...])


def _odd_out(x, o, sz, wo, g, b):
    n = x.shape[0]
    tm = min(512, n)
    row = pl.BlockSpec((tm, D_MODEL), lambda i: (i, 0))
    return pl.pallas_call(
        _odd_out_kernel,
        grid=(n // tm,),
        in_specs=[row, row, row, _full(wo.shape), _full(g.shape), _full(b.shape)],
        out_specs=row,
        out_shape=jax.ShapeDtypeStruct((n, D_MODEL), F32),
        compiler_params=pltpu.CompilerParams(
            dimension_semantics=("arbitrary",), vmem_limit_bytes=VMEM_LIMIT),
        name="odd_out",
    )(x, o, sz, wo, g, b)


def _swap_halves(w):
    half = w.shape[-1] // 2
    return jnp.concatenate([w[..., half:], w[..., :half]], axis=-1)


def _layout_w_in0(w):
    h, gc, gb, za, qa, kva, kpe, zb = jnp.split(
        w, [512, 1024, 1536, 2048, 2560, 2816, 2848], axis=1)
    pad = jnp.zeros((w.shape[0], HEAD_PAD - ROPE_DIM), w.dtype)
    cols = [h, gc, gb, za, zb, qa, kva, kpe, pad, _swap_halves(kpe), pad]
    return jnp.concatenate(cols, axis=1).astype(BF16)


def _layout_w_qb(w):
    w = w.reshape(Q_RANK, H_MLA, NOPE_DIM + ROPE_DIM)
    nope, pe = w[..., :NOPE_DIM], w[..., NOPE_DIM:]
    z32 = jnp.zeros((Q_RANK, H_MLA, HEAD_PAD - NOPE_DIM - ROPE_DIM), w.dtype)
    z96 = jnp.zeros((Q_RANK, H_MLA, HEAD_PAD - ROPE_DIM), w.dtype)
    wq = jnp.concatenate([pe, nope, z32], axis=-1).reshape(Q_RANK, H_MLA * HEAD_PAD)
    wqs = jnp.concatenate([_swap_halves(pe), z96], axis=-1).reshape(Q_RANK, H_MLA * HEAD_PAD)
    return wq.astype(BF16), wqs.astype(BF16)


def _layout_w_uk(w_uk):
    zl = jnp.zeros((KV_RANK, H_MLA, ROPE_DIM), w_uk.dtype)
    zr = jnp.zeros((KV_RANK, H_MLA, HEAD_PAD - NOPE_DIM - ROPE_DIM), w_uk.dtype)
    return jnp.concatenate([zl, w_uk, zr], axis=-1).reshape(KV_RANK, H_MLA * HEAD_PAD)


def _rope_tables(pos):
    half = ROPE_DIM // 2
    freq = ROPE_THETA ** (-jnp.arange(half, dtype=F32) / half)
    ang = pos.astype(F32)[:, None] * freq[None, :]
    cos, sin = jnp.cos(ang), jnp.sin(ang)
    n = pos.shape[0]
    ones = jnp.ones((n, NOPE_DIM), F32)
    zc = jnp.zeros((n, HEAD_PAD - NOPE_DIM - ROPE_DIM), F32)
    zs = jnp.zeros((n, HEAD_PAD - ROPE_DIM), F32)
    tc = jnp.concatenate([cos, cos, ones, zc], axis=1)
    ts = jnp.concatenate([-sin, sin, zs], axis=1)
    return tc, ts


def _group_major_cols(w):
    lead = w.shape[0]
    return w.reshape(lead, N_KV_SB, G_SB, HD_SB).transpose(0, 2, 1, 3).reshape(lead, MIX1)


def _layout_w_in1_sample(w):
    kv_w = N_KV_SB * HD_SB
    q, k, v, z = jnp.split(w, [MIX1, MIX1 + kv_w, MIX1 + 2 * kv_w], axis=1)
    return jnp.concatenate([_group_major_cols(q), k, v, _group_major_cols(z)], axis=1).astype(BF16)


def kernel(x_prompt, x_sample, state_conv, cache_ckv, cache_kpe, cache_k, cache_v, page_table,
           w_in0, conv_w, q_norm, w_qb, kv_norm, w_uk, w_uv, w_out0, ln0_g, ln0_b,
           w_in1, w_out1, ln1_g, ln1_b):
    bsz, s_len, _ = x_prompt.shape
    dbsz, t_len, _ = x_sample.shape
    assert t_len == 1, "the sample group is one new token per sequence"
    n_pages = page_table.shape[1]
    p_len = n_pages * PAGE_SIZE
    n_phys = cache_ckv.shape[1]
    kv_w = N_KV_SB * HD_SB
    rows_p = bsz * s_len

    wa = _layout_w_in0(w_in0[0])
    wq, wqs = _layout_w_qb(w_qb[0])
    wuk_pad = _layout_w_uk(w_uk[0])
    wuv = w_uv[0].reshape(KV_RANK, H_MLA * V_DIM)
    wkv = jnp.concatenate([wuk_pad, wuv], axis=1).astype(BF16)
    wukt = wuk_pad.T.astype(BF16)
    wuv_b = wuv.astype(BF16)
    qn = q_norm[0].reshape(1, Q_RANK)
    kvn = kv_norm[0].reshape(1, KV_RANK)
    cw = conv_w[0]
    wo0 = w_out0[0].astype(BF16)
    g0, b0 = ln0_g[0].reshape(1, D_MODEL), ln0_b[0].reshape(1, D_MODEL)
    g1, b1 = ln1_g[0].reshape(1, D_MODEL), ln1_b[0].reshape(1, D_MODEL)
    w1p = w_in1[0].astype(BF16)
    w1s = _layout_w_in1_sample(w_in1[0])
    wo1 = w_out1[0].astype(BF16)
    wo1_gm = _group_major_cols(w_out1[0].T).T.astype(BF16)
    tc_p, ts_p = _rope_tables(jnp.arange(s_len))
    tc_s, ts_s = _rope_tables(jnp.full((dbsz,), p_len))

    ya, szb, q, k, v, ckv_p, kpe_p, tail = _even_in_prompt(
        x_prompt, wa, wq, wqs, wkv, qn, kvn, cw, tc_p, ts_p,
        tc_p * (MLA_SCALE * LOG2E), ts_p * (MLA_SCALE * LOG2E))
    attn = _mla_prompt(q, k, v)
    x1, q1, k1t, v1t, sz1, ka, kb, vd = _mid(
        x_prompt.reshape(rows_p, D_MODEL), ya.reshape(rows_p, D_CONV),
        attn.reshape(rows_p, H_MLA * V_DIM), szb.reshape(rows_p, H_MLA * V_DIM),
        wo0, g0, b0, w1p, s_len, BF16)
    o1 = _sb_prompt(q1.reshape(bsz, s_len, MIX1), ka.reshape(bsz, s_len, N_KV_SB * LANES),
                    kb.reshape(bsz, s_len, N_KV_SB * LANES), vd.reshape(bsz, s_len, N_KV_SB * LANES))
    y_prompt = _odd_out(x1, o1.reshape(rows_p, MIX1), sz1, wo1, g1, b1).reshape(bsz, s_len, D_MODEL)

    xs = x_sample.reshape(dbsz, D_MODEL)
    ya_s, szb_s, q_s, ckv_s, kpe_s, u_s = _even_in_sample(
        xs, wa, wq, wqs, qn, kvn, cw, tc_s, ts_s, tc_s * MLA_SCALE, ts_s * MLA_SCALE,
        state_conv[0, :, 0, :], state_conv[0, :, 1, :])
    cache_kpet = jnp.transpose(cache_kpe[0], (0, 2, 1))
    cache_kt = jnp.transpose(cache_k[0], (0, 2, 3, 1)).reshape(n_phys, kv_w, PAGE_SIZE)
    cache_vt = jnp.transpose(cache_v[0], (0, 2, 3, 1)).reshape(n_phys, kv_w, PAGE_SIZE)
    attn_s = _mla_sample(page_table, q_s, wukt, wuv_b, ckv_s, kpe_s, cache_ckv[0], cache_kpet)
    x1_s, q1_s, k1_s, v1_s, sz1_s = _mid(xs, ya_s, attn_s, szb_s, wo0, g0, b0, w1s, None, F32)
    o1_s = _sb_sample(page_table, q1_s, cache_kt, cache_vt)
    y_sample = _odd_out(x1_s, o1_s, sz1_s, wo1_gm, g1, b1).reshape(dbsz, 1, D_MODEL)

    conv_prompt = tail[:, SUBLANES - (CONV_W - 1):, :][None]
    conv_sample = jnp.stack([state_conv[0, :, 1, :], u_s], axis=1)[None]

    def from_feature_major(t):
        return jnp.transpose(t.reshape(bsz, N_KV_SB, HD_SB, s_len), (0, 3, 1, 2))[None]

    return (y_prompt, y_sample, conv_prompt, conv_sample,
            ckv_p[None], ckv_s.reshape(1, dbsz, 1, KV_RANK),
            kpe_p[None], kpe_s.reshape(1, dbsz, 1, ROPE_DIM),
            from_feature_major(k1t), k1_s.reshape(1, dbsz, 1, N_KV_SB, HD_SB),
            from_feature_major(v1t), v1_s.reshape(1, dbsz, 1, N_KV_SB, HD_SB))
```
